```python
import jax
import jax.numpy as jnp
from jax import lax
import numpy as np

D_MODEL = 4096
BATCH = 16
SEQ = 256
DEPTH = 4
DEC_BATCH = 2
DEC_SEQ = 4096
PAST_LEN = 512

GRID_W = 64
HEAD_DIM = 128
N_HEADS = 16
N_KV = 4
C_ATTN = N_HEADS * HEAD_DIM
ROT_AXIS = HEAD_DIM // 2
ROPE_THETA = 10000.0
Q_BLOCK = 128
C_CONV = D_MODEL // 4
CONV_WIDTH = 31
C_RWKV = D_MODEL - C_ATTN - C_CONV
RWKV_HEAD = 64
H_RWKV = C_RWKV // RWKV_HEAD
LR_W = 64
LR_A = 64
LR_G = 128
RW_DIR = LR_W + LR_A
RW_SHIFT = 3 * C_RWKV + RW_DIR
D_FF = -(-8 * D_MODEL // (3 * 256)) * 256
N_MOD = 6
K_OFF = C_ATTN
V_OFF = K_OFF + N_KV * HEAD_DIM
CONV_OFF = V_OFF + N_KV * HEAD_DIM
RWKV_OFF = CONV_OFF + 2 * C_CONV
N_IN = RWKV_OFF + 3 * C_RWKV + 2 * RW_DIR + LR_G

NORM_EPS = 1e-6
LN_EPS = 1e-5
RWKV_GN_EPS = 64e-5

kernel_name = 'hymba_style_conv_gqa_rwkv7_diffusion_step'


def rms_norm(x, g):
    x32 = x.astype(jnp.float32)
    y = x32 * lax.rsqrt(jnp.mean(x32 * x32, axis=-1, keepdims=True) + NORM_EPS)
    return (y * g.astype(jnp.float32)).astype(x.dtype)


def layer_norm(x, g, b, eps):
    x32 = x.astype(jnp.float32)
    xc = x32 - jnp.mean(x32, axis=-1, keepdims=True)
    var = jnp.mean(xc * xc, axis=-1, keepdims=True)
    return xc * lax.rsqrt(var + eps) * g.astype(jnp.float32) + b.astype(jnp.float32)


def axial_rope(n_tok):
    rows = n_tok // GRID_W
    row = jnp.repeat(jnp.arange(rows, dtype=jnp.int32), GRID_W).astype(jnp.float32)
    col = jnp.tile(jnp.arange(GRID_W, dtype=jnp.int32), rows).astype(jnp.float32)
    inv_freq = jnp.power(ROPE_THETA, -jnp.arange(0, ROT_AXIS, 2, dtype=jnp.float32) / ROT_AXIS)
    ang_r = row[:, None] * inv_freq[None, :]
    ang_c = col[:, None] * inv_freq[None, :]
    return (jnp.cos(ang_r), jnp.sin(ang_r), jnp.cos(ang_c), jnp.sin(ang_c))


def rotate(x, cos, sin):
    m = x.shape[-1] // 2
    x1, x2 = x[..., :m], x[..., m:]
    cs = cos[None, :, None, :].astype(x.dtype)
    sn = sin[None, :, None, :].astype(x.dtype)
    return jnp.concatenate([x1 * cs - x2 * sn, x2 * cs + x1 * sn], axis=-1)


def apply_axial_rope(x, rope):
    cr, sr, cc, sc = rope
    return jnp.concatenate([rotate(x[..., :ROT_AXIS], cr, sr),
                            rotate(x[..., ROT_AXIS:], cc, sc)], axis=-1)


def attend(q, k, v):
    b, tq, nh, hd = q.shape
    nkv = k.shape[2]
    grp = nh // nkv
    nb = tq // Q_BLOCK
    qb = (q * (hd ** -0.5)).reshape(b, nb, Q_BLOCK, nkv, grp, hd).transpose(1, 0, 2, 3, 4, 5)

    def one_block(qblk):
        s = jnp.einsum('bqhgd,bshd->bhgqs', qblk, k, preferred_element_type=jnp.float32)
        p = jax.nn.softmax(s, axis=-1).astype(v.dtype)
        return jnp.einsum('bhgqs,bshd->bqhgd', p, v)

    o = lax.map(one_block, qb)
    return o.transpose(1, 0, 2, 3, 4, 5).reshape(b, tq, nh * hd)


def depthwise_conv(x, w, bias):
    y = lax.conv_general_dilated(x, w[:, None, :].astype(x.dtype), window_strides=(1,),
                                 padding=[(CONV_WIDTH // 2, CONV_WIDTH // 2)],
                                 dimension_numbers=('NWC', 'WIO', 'NWC'),
                                 feature_group_count=x.shape[-1])
    return y + bias


def conv_module(za, zb, lp):
    u = za * jax.nn.sigmoid(zb)
    u = depthwise_conv(u, lp['conv_w'], lp['conv_b'])
    u = layer_norm(u, lp['conv_ln_g'], lp['conv_ln_b'], LN_EPS)
    return jax.nn.silu(u).astype(za.dtype)


def token_shift(z, reverse):
    if reverse:
        return jnp.pad(z[:, 1:], ((0, 0), (0, 1), (0, 0)))
    return jnp.pad(z[:, :-1], ((0, 0), (1, 0), (0, 0)))


def wkv_scan(s0, r, w, k, v, kk, a, reverse):
    xs = tuple(jnp.swapaxes(t.astype(jnp.float32), 0, 1) for t in (r, w, k, v, kk, a))

    def step(s, inp):
        r_t, w_t, k_t, v_t, kk_t, a_t = inp
        sa = jnp.einsum('bhvk,bhk->bhv', s, -kk_t)
        s = (s * w_t[:, :, None, :] + sa[..., None] * (kk_t * a_t)[:, :, None, :]
             + v_t[..., None] * k_t[:, :, None, :])
        return s, jnp.einsum('bhvk,bhk->bhv', s, r_t)

    s_fin, ys = lax.scan(step, s0.astype(jnp.float32), xs, reverse=reverse)
    return s_fin, jnp.swapaxes(ys, 0, 1)


def rwkv_mixer(zr, s0_fwd, s0_bwd, lp):
    b, t, _ = zr.shape
    heads = lambda u: u.reshape(b, t, H_RWKV, RWKV_HEAD)
    rkv = zr[..., :3 * C_RWKV]
    g_low = zr[..., 3 * C_RWKV + 2 * RW_DIR:]
    s0s = (s0_fwd, s0_bwd)
    y_sum = 0.0
    bonus = 0.0
    finals = []
    for d, rev in ((0, False), (1, True)):
        lo = 3 * C_RWKV + d * RW_DIR
        zd = jnp.concatenate([rkv, zr[..., lo:lo + RW_DIR]], axis=-1)
        zd = zd + lp['rw_mu'][d] * (token_shift(zd, rev) - zd)
        r = zd[..., :C_RWKV]
        k = zd[..., C_RWKV:2 * C_RWKV]
        v = zd[..., 2 * C_RWKV:3 * C_RWKV]
        wl = zd[..., 3 * C_RWKV:3 * C_RWKV + LR_W]
        al = zd[..., 3 * C_RWKV + LR_W:]
        w_log = -jax.nn.softplus(-(lp['rw_w0'][d] + jnp.tanh(wl) @ lp['rw_w2'][d])) - 0.5
        decay = jnp.exp(-jnp.exp(w_log.astype(jnp.float32)))
        a = jax.nn.sigmoid(lp['rw_a0'][d] + al @ lp['rw_a2'][d])
        kk = heads(k * lp['rw_k_k']).astype(jnp.float32)
        kk = kk / jnp.maximum(jnp.sqrt(jnp.sum(kk * kk, axis=-1, keepdims=True)), 1e-12)
        k = k * (1.0 + (a - 1.0) * lp['rw_k_a'])
        rh, kh, vh = heads(r), heads(k), heads(v)
        s_fin, y = wkv_scan(s0s[d], rh, heads(decay), kh, vh, kk, heads(a), rev)
        y_sum = y_sum + y
        bonus = bonus + jnp.sum(rh * kh * lp['rw_r_k'], axis=-1, keepdims=True) * vh
        finals.append(s_fin)
    y = layer_norm(y_sum, lp['rw_ln_g'].reshape(H_RWKV, RWKV_HEAD),
                   lp['rw_ln_b'].reshape(H_RWKV, RWKV_HEAD), RWKV_GN_EPS)
    y = (y + bonus.astype(jnp.float32)).astype(zr.dtype).reshape(b, t, C_RWKV)
    g = jax.nn.sigmoid(g_low) @ lp['rw_g2']
    return y * g, finals[0], finals[1]


def mixer(h, lp, rope, ctx):
    b, t, _ = h.shape
    z = h @ lp['w_in']
    q = rms_norm(z[..., :K_OFF].reshape(b, t, N_HEADS, HEAD_DIM), lp['q_norm'])
    k = rms_norm(z[..., K_OFF:V_OFF].reshape(b, t, N_KV, HEAD_DIM), lp['k_norm'])
    v = z[..., V_OFF:CONV_OFF].reshape(b, t, N_KV, HEAD_DIM)
    if rope is not None:
        q = apply_axial_rope(q, rope)
        k = apply_axial_rope(k, rope)
    if ctx is None:
        k_all, v_all = k, v
        s0f = jnp.zeros((b, H_RWKV, RWKV_HEAD, RWKV_HEAD), jnp.float32)
        s0b = s0f
    else:
        ck, cv, s0f, s0b = ctx
        k_all = jnp.concatenate([ck.astype(k.dtype), k], axis=1)
        v_all = jnp.concatenate([cv.astype(v.dtype), v], axis=1)
    attn = attend(q, k_all, v_all)
    conv = conv_module(z[..., CONV_OFF:CONV_OFF + C_CONV], z[..., CONV_OFF + C_CONV:RWKV_OFF], lp)
    rw, sf, sb = rwkv_mixer(z[..., RWKV_OFF:], s0f, s0b, lp)
    out = jnp.concatenate([attn, conv, rw], axis=-1) @ lp['w_out']
    return out, (k, v, sf, sb)


def layer(x, mod, lp, rope, ctx):
    sh1, sc1, g1, sh2, sc2, g2 = jnp.split(mod, N_MOD, axis=-1)
    h = rms_norm(x, lp['norm1']) * (1.0 + sc1) + sh1
    m, ctx_out = mixer(h, lp, rope, ctx)
    x = x + g1 * m
    h = rms_norm(x, lp['norm2']) * (1.0 + sc2) + sh2
    f = (jax.nn.silu(h @ lp['ffn_gate']) * (h @ lp['ffn_up'])) @ lp['ffn_down']
    return x + g2 * f, ctx_out


def setup_inputs(seed: int = 0) -> dict:
    key = jax.random.key(seed)
    ks = jax.random.split(key, 40)
    f32 = jnp.float32
    nrm = lambda k, shape, std: jax.random.normal(k, shape, f32) * std
    D = D_MODEL
    return {
        'x_prompt': nrm(ks[0], (BATCH, SEQ, D), 1.0),
        'x_sample': nrm(ks[1], (DEC_BATCH, DEC_SEQ, D), 1.0),
        'c': nrm(ks[2], (DEC_BATCH, D), 1.0),
        'cache_k': nrm(ks[3], (DEC_BATCH, DEPTH, PAST_LEN, N_KV, HEAD_DIM), 1.0),
        'cache_v': nrm(ks[4], (DEC_BATCH, DEPTH, PAST_LEN, N_KV, HEAD_DIM), 1.0),
        'state_fwd': nrm(ks[5], (DEC_BATCH, DEPTH, H_RWKV, RWKV_HEAD, RWKV_HEAD), 0.5),
        'state_bwd': nrm(ks[6], (DEC_BATCH, DEPTH, H_RWKV, RWKV_HEAD, RWKV_HEAD), 0.5),
        'c_ctx': nrm(ks[7], (D,), 1.0),
        'w_mod': nrm(ks[8], (DEPTH, D, N_MOD * D), 0.3 * D ** -0.5),
        'b_mod': nrm(ks[9], (DEPTH, N_MOD * D), 0.02),
        'norm1': 1.0 + nrm(ks[10], (DEPTH, D), 0.02),
        'norm2': 1.0 + nrm(ks[11], (DEPTH, D), 0.02),
        'w_in': nrm(ks[12], (DEPTH, D, N_IN), D ** -0.5),
        'q_norm': 1.0 + nrm(ks[13], (DEPTH, HEAD_DIM), 0.02),
        'k_norm': 1.0 + nrm(ks[14], (DEPTH, HEAD_DIM), 0.02),
        'conv_w': nrm(ks[15], (DEPTH, CONV_WIDTH, C_CONV), CONV_WIDTH ** -0.5),
        'conv_b': nrm(ks[16], (DEPTH, C_CONV), 0.02),
        'conv_ln_g': 1.0 + nrm(ks[17], (DEPTH, C_CONV), 0.02),
        'conv_ln_b': nrm(ks[18], (DEPTH, C_CONV), 0.02),
        'rw_mu': jax.random.uniform(ks[19], (DEPTH, 2, RW_SHIFT), f32),
        'rw_w0': -2.0 + nrm(ks[20], (DEPTH, 2, C_RWKV), 0.5),
        'rw_w2': nrm(ks[21], (DEPTH, 2, LR_W, C_RWKV), 0.5 * LR_W ** -0.5),
        'rw_a0': nrm(ks[22], (DEPTH, 2, C_RWKV), 0.1),
        'rw_a2': nrm(ks[23], (DEPTH, 2, LR_A, C_RWKV), 0.5 * LR_A ** -0.5),
        'rw_g2': nrm(ks[24], (DEPTH, LR_G, C_RWKV), LR_G ** -0.5),
        'rw_k_k': 0.85 + nrm(ks[25], (DEPTH, C_RWKV), 0.05),
        'rw_k_a': 1.0 + nrm(ks[26], (DEPTH, C_RWKV), 0.05),
        'rw_r_k': nrm(ks[27], (DEPTH, H_RWKV, RWKV_HEAD), 0.1),
        'rw_ln_g': 1.0 + nrm(ks[28], (DEPTH, C_RWKV), 0.02),
        'rw_ln_b': nrm(ks[29], (DEPTH, C_RWKV), 0.02),
        'w_out': nrm(ks[30], (DEPTH, D, D), D ** -0.5),
        'ffn_gate': nrm(ks[31], (DEPTH, D, D_FF), D ** -0.5),
        'ffn_up': nrm(ks[32], (DEPTH, D, D_FF), D ** -0.5),
        'ffn_down': nrm(ks[33], (DEPTH, D_FF, D), D_FF ** -0.5),
    }


def reference(x_prompt, x_sample, c, cache_k, cache_v, state_fwd, state_bwd, c_ctx,
              w_mod, b_mod, norm1, norm2, w_in, q_norm, k_norm, conv_w, conv_b,
              conv_ln_g, conv_ln_b, rw_mu, rw_w0, rw_w2, rw_a0, rw_a2, rw_g2,
              rw_k_k, rw_k_a, rw_r_k, rw_ln_g, rw_ln_b, w_out, ffn_gate, ffn_up, ffn_down):
    rope = axial_rope(x_sample.shape[1])
    silu_ctx = jax.nn.silu(c_ctx)[None, None, :]
    silu_c = jax.nn.silu(c)[:, None, :]
    y_p = x_prompt
    y_s = x_sample
    ks_out, vs_out, sf_out, sb_out = [], [], [], []
    for l in range(DEPTH):
        lp = {
            'norm1': norm1[l], 'norm2': norm2[l], 'w_in': w_in[l],
            'q_norm': q_norm[l], 'k_norm': k_norm[l],
            'conv_w': conv_w[l], 'conv_b': conv_b[l],
            'conv_ln_g': conv_ln_g[l], 'conv_ln_b': conv_ln_b[l],
            'rw_mu': rw_mu[l], 'rw_w0': rw_w0[l], 'rw_w2': rw_w2[l],
            'rw_a0': rw_a0[l], 'rw_a2': rw_a2[l], 'rw_g2': rw_g2[l],
            'rw_k_k': rw_k_k[l], 'rw_k_a': rw_k_a[l], 'rw_r_k': rw_r_k[l],
            'rw_ln_g': rw_ln_g[l], 'rw_ln_b': rw_ln_b[l],
            'w_out': w_out[l], 'ffn_gate': ffn_gate[l], 'ffn_up': ffn_up[l],
            'ffn_down': ffn_down[l],
        }
        mod_p = silu_ctx @ w_mod[l] + b_mod[l]
        mod_s = silu_c @ w_mod[l] + b_mod[l]
        y_p, (k_c, v_c, sf_c, sb_c) = layer(y_p, mod_p, lp, None, None)
        ks_out.append(k_c)
        vs_out.append(v_c)
        sf_out.append(sf_c)
        sb_out.append(sb_c)
        y_s, _ = layer(y_s, mod_s, lp, rope,
                       (cache_k[:, l], cache_v[:, l], state_fwd[:, l], state_bwd[:, l]))
    new_cache_k = jnp.stack(ks_out, axis=1)
    new_cache_v = jnp.stack(vs_out, axis=1)
    new_state_fwd = jnp.stack(sf_out, axis=1)
    new_state_bwd = jnp.stack(sb_out, axis=1)
    return (y_p, y_s, new_cache_k, new_cache_v, new_state_fwd, new_state_bwd)
```

```python
import functools

import jax
import jax.numpy as jnp
from jax import lax
from jax.experimental import pallas as pl
from jax.experimental.pallas import tpu as pltpu

F32 = jnp.float32
BF16 = jnp.bfloat16

GRID_W = 64
ROPE_THETA = 10000.0
NORM_EPS = 1e-6
LN_EPS = 1e-5
RWKV_GN_EPS = 64e-5

LANES = 128
SUBLANES = 8
ROW_BLOCK = 256
CONV_HALO = 16
FF_ALIGN = 1024
VMEM_LIMIT = 56 * 1024 * 1024


def _params(*sem):
    return pltpu.CompilerParams(dimension_semantics=sem, vmem_limit_bytes=VMEM_LIMIT)


def _silu(x):
    return x * jax.nn.sigmoid(x)


def _mod_body(c_ref, w_ref, b_ref, o_ref):
    s = _silu(c_ref[...]).astype(BF16)
    o_ref[...] = jnp.dot(s, w_ref[...].astype(BF16), preferred_element_type=F32) + b_ref[...]


def _modulation(cvec, w_mod, b_mod):
    n_layer, d, n = w_mod.shape
    tn = 512
    return pl.pallas_call(
        _mod_body,
        grid=(n_layer, n // tn),
        in_specs=[pl.BlockSpec((SUBLANES, d), lambda l, j: (0, 0)),
                  pl.BlockSpec((None, d, tn), lambda l, j: (l, 0, j)),
                  pl.BlockSpec((None, 1, tn), lambda l, j: (l, 0, j))],
        out_specs=pl.BlockSpec((None, SUBLANES, tn), lambda l, j: (l, 0, j)),
        out_shape=jax.ShapeDtypeStruct((n_layer, SUBLANES, n), F32),
        compiler_params=_params("parallel", "parallel"),
        name="modulation",
    )(cvec, w_mod, b_mod.reshape(n_layer, 1, n))


def _norm_mod_body(x_ref, g_ref, mod_ref, o_ref, *, shift_row, scale_row):
    x = x_ref[...]
    y = x * lax.rsqrt(jnp.mean(x * x, axis=-1, keepdims=True) + NORM_EPS) * g_ref[...]
    sc = mod_ref[scale_row:scale_row + 1, :]
    sh = mod_ref[shift_row:shift_row + 1, :]
    o_ref[...] = (y * (1.0 + sc) + sh).astype(o_ref.dtype)


def _norm_mod(x, g, mod, group_of_row, shift_row, scale_row):
    nt, d = x.shape
    return pl.pallas_call(
        functools.partial(_norm_mod_body, shift_row=shift_row, scale_row=scale_row),
        grid=(nt // ROW_BLOCK,),
        in_specs=[pl.BlockSpec((ROW_BLOCK, d), lambda i: (i, 0)),
                  pl.BlockSpec((1, d), lambda i: (0, 0)),
                  pl.BlockSpec((None, 6, d), lambda i: (group_of_row(i * ROW_BLOCK), 0, 0))],
        out_specs=pl.BlockSpec((ROW_BLOCK, d), lambda i: (i, 0)),
        out_shape=jax.ShapeDtypeStruct((nt, d), BF16),
        compiler_params=_params("parallel"),
        name="norm_mod",
    )(x, g.reshape(1, d), mod)


def _mm_body(a_ref, b_ref, o_ref):
    o_ref[...] = jnp.dot(a_ref[...], b_ref[...], preferred_element_type=F32).astype(o_ref.dtype)


def _matmul(a, b, out_dtype, tm, tn):
    m, k = a.shape
    _, n = b.shape
    return pl.pallas_call(
        _mm_body,
        grid=(m // tm, n // tn),
        in_specs=[pl.BlockSpec((tm, k), lambda i, j: (i, 0)),
                  pl.BlockSpec((k, tn), lambda i, j: (0, j))],
        out_specs=pl.BlockSpec((tm, tn), lambda i, j: (i, j)),
        out_shape=jax.ShapeDtypeStruct((m, n), out_dtype),
        compiler_params=_params("parallel", "arbitrary"),
        name="matmul",
    )(a, b)


def _mm_resid_body(a_ref, b_ref, x_ref, mod_ref, o_ref, *, gate_row):
    acc = jnp.dot(a_ref[...], b_ref[...], preferred_element_type=F32)
    o_ref[...] = x_ref[...] + mod_ref[gate_row:gate_row + 1, :] * acc


def _matmul_gated_residual(a, b, x, mod, group_of_row, gate_row, tm, tn):
    m, k = a.shape
    _, n = b.shape
    return pl.pallas_call(
        functools.partial(_mm_resid_body, gate_row=gate_row),
        grid=(m // tm, n // tn),
        in_specs=[pl.BlockSpec((tm, k), lambda i, j: (i, 0)),
                  pl.BlockSpec((k, tn), lambda i, j: (0, j)),
                  pl.BlockSpec((tm, tn), lambda i, j: (i, j)),
                  pl.BlockSpec((None, 6, tn), lambda i, j: (group_of_row(i * tm), 0, j))],
        out_specs=pl.BlockSpec((tm, tn), lambda i, j: (i, j)),
        out_shape=jax.ShapeDtypeStruct((m, n), F32),
        compiler_params=_params("parallel", "arbitrary"),
        name="matmul_gated_residual",
    )(a, b, x, mod)


def _mm_swiglu_body(a_ref, wg_ref, wu_ref, o_ref):
    a = a_ref[...]
    g = jnp.dot(a, wg_ref[...], preferred_element_type=F32)
    u = jnp.dot(a, wu_ref[...], preferred_element_type=F32)
    o_ref[...] = (_silu(g) * u).astype(o_ref.dtype)


def _matmul_swiglu(a, wg, wu, tm, tn):
    m, k = a.shape
    _, n = wg.shape
    return pl.pallas_call(
        _mm_swiglu_body,
        grid=(m // tm, n // tn),
        in_specs=[pl.BlockSpec((tm, k), lambda i, j: (i, 0)),
                  pl.BlockSpec((k, tn), lambda i, j: (0, j)),
                  pl.BlockSpec((k, tn), lambda i, j: (0, j))],
        out_specs=pl.BlockSpec((tm, tn), lambda i, j: (i, j)),
        out_shape=jax.ShapeDtypeStruct((m, n), BF16),
        compiler_params=_params("parallel", "arbitrary"),
        name="matmul_swiglu",
    )(a, wg, wu)


def _qk_body(z_ref, qg_ref, kg_ref, cos_ref, sin_ref, q_ref, k_ref, kn_ref, v_ref,
             *, n_heads, n_kv, hd):
    cos = cos_ref[...]
    sin = sin_ref[...]
    lane = lax.broadcasted_iota(jnp.int32, (1, hd), 1)
    first_half = (lane // (hd // 4)) % 2 == 0

    def normed(col, g):
        x = z_ref[:, col:col + hd]
        return x * lax.rsqrt(jnp.mean(x * x, axis=-1, keepdims=True) + NORM_EPS) * g

    def rotated(y):
        swapped = jnp.where(first_half, pltpu.roll(y, hd - hd // 4, axis=1),
                            pltpu.roll(y, hd // 4, axis=1))
        return y * cos + swapped * sin

    for h in range(n_heads):
        y = rotated(normed(h * hd, qg_ref[...]))
        q_ref[:, h * hd:(h + 1) * hd] = (y * (hd ** -0.5)).astype(q_ref.dtype)
    for h in range(n_kv):
        y = normed((n_heads + h) * hd, kg_ref[...])
        kn_ref[:, h * hd:(h + 1) * hd] = y
        k_ref[:, h * hd:(h + 1) * hd] = rotated(y).astype(k_ref.dtype)
    v_col = (n_heads + n_kv) * hd
    v_ref[...] = z_ref[:, v_col:v_col + n_kv * hd].astype(v_ref.dtype)


def _qk_norm_rope(z, q_g, k_g, cos_tab, sin_tab, rope_block_of_row, n_heads, n_kv, hd):
    nt = z.shape[0]
    qkv_cols = (n_heads + 2 * n_kv) * hd
    return pl.pallas_call(
        functools.partial(_qk_body, n_heads=n_heads, n_kv=n_kv, hd=hd),
        grid=(nt // ROW_BLOCK,),
        in_specs=[pl.BlockSpec((ROW_BLOCK, qkv_cols), lambda i: (i, 0)),
                  pl.BlockSpec((1, hd), lambda i: (0, 0)),
                  pl.BlockSpec((1, hd), lambda i: (0, 0)),
                  pl.BlockSpec((ROW_BLOCK, hd), lambda i: (rope_block_of_row(i * ROW_BLOCK), 0)),
                  pl.BlockSpec((ROW_BLOCK, hd), lambda i: (rope_block_of_row(i * ROW_BLOCK), 0))],
        out_specs=[pl.BlockSpec((ROW_BLOCK, n_heads * hd), lambda i: (i, 0)),
                   pl.BlockSpec((ROW_BLOCK, n_kv * hd), lambda i: (i, 0)),
                   pl.BlockSpec((ROW_BLOCK, n_kv * hd), lambda i: (i, 0)),
                   pl.BlockSpec((ROW_BLOCK, n_kv * hd), lambda i: (i, 0))],
        out_shape=[jax.ShapeDtypeStruct((nt, n_heads * hd), BF16),
                   jax.ShapeDtypeStruct((nt, n_kv * hd), BF16),
                   jax.ShapeDtypeStruct((nt, n_kv * hd), F32),
                   jax.ShapeDtypeStruct((nt, n_kv * hd), BF16)],
        compiler_params=_params("parallel"),
        name="qk_norm_rope",
    )(z, q_g.reshape(1, hd), k_g.reshape(1, hd), cos_tab, sin_tab)


def _attn_body(q_ref, k_ref, v_ref, o_ref, *, grp, hd):
    k = k_ref[...]
    v = v_ref[...]
    for h in range(grp):
        q = q_ref[:, h * hd:(h + 1) * hd]
        s = lax.dot_general(q, k, (((1,), (1,)), ((), ())), preferred_element_type=F32)
        p = jnp.exp(s - jnp.max(s, axis=-1, keepdims=True))
        denom = jnp.sum(p, axis=-1, keepdims=True)
        o = jnp.dot(p.astype(v.dtype), v, preferred_element_type=F32) / denom
        o_ref[:, h * hd:(h + 1) * hd] = o.astype(o_ref.dtype)


def _attention(q, k, v, grp, hd):
    b, tq, qc = q.shape
    _, tk, kc = k.shape
    n_kv = kc // hd
    bq = min(ROW_BLOCK, tq)
    return pl.pallas_call(
        functools.partial(_attn_body, grp=grp, hd=hd),
        grid=(b, n_kv, tq // bq),
        in_specs=[pl.BlockSpec((None, bq, grp * hd), lambda bi, g, i: (bi, i, g)),
                  pl.BlockSpec((None, tk, hd), lambda bi, g, i: (bi, 0, g)),
                  pl.BlockSpec((None, tk, hd), lambda bi, g, i: (bi, 0, g))],
        out_specs=pl.BlockSpec((None, bq, grp * hd), lambda bi, g, i: (bi, i, g)),
        out_shape=jax.ShapeDtypeStruct((b, tq, qc), BF16),
        compiler_params=_params("parallel", "parallel", "arbitrary"),
        name="attention",
    )(q, k, v)


def _conv_body(za_ref, zb_ref, pa_ref, pb_ref, na_ref, nb_ref, w_ref, cb_ref, g_ref, b_ref,
               o_ref, buf_ref, *, width, first_of_seq, last_of_seq):
    i = pl.program_id(0)
    row0 = i * ROW_BLOCK
    keep_prev = jnp.where(first_of_seq(row0), 0.0, 1.0)
    keep_next = jnp.where(last_of_seq(row0), 0.0, 1.0)
    buf_ref[0:CONV_HALO, :] = pa_ref[...] * jax.nn.sigmoid(pb_ref[...]) * keep_prev
    buf_ref[CONV_HALO:CONV_HALO + ROW_BLOCK, :] = za_ref[...] * jax.nn.sigmoid(zb_ref[...])
    buf_ref[CONV_HALO + ROW_BLOCK:, :] = na_ref[...] * jax.nn.sigmoid(nb_ref[...]) * keep_next
    half = width // 2
    acc = jnp.zeros(o_ref.shape, F32) + cb_ref[...]
    for j in range(width):
        start = CONV_HALO - half + j
        acc = acc + buf_ref[start:start + ROW_BLOCK, :] * w_ref[j:j + 1, :]
    xc = acc - jnp.mean(acc, axis=-1, keepdims=True)
    var = jnp.mean(xc * xc, axis=-1, keepdims=True)
    y = xc * lax.rsqrt(var + LN_EPS) * g_ref[...] + b_ref[...]
    o_ref[...] = _silu(y).astype(o_ref.dtype)


def _conv_module(z, col_a, conv_w, conv_b, ln_g, ln_b, first_of_seq, last_of_seq):
    nt = z.shape[0]
    width, c = conv_w.shape
    ca = col_a // c
    hb = ROW_BLOCK // CONV_HALO
    n_halo = nt // CONV_HALO
    cur = lambda off: pl.BlockSpec((ROW_BLOCK, c), lambda i: (i, ca + off))
    prev = lambda off: pl.BlockSpec((CONV_HALO, c), lambda i: (jnp.maximum(i * hb - 1, 0), ca + off))
    nxt = lambda off: pl.BlockSpec((CONV_HALO, c),
                                   lambda i: (jnp.minimum((i + 1) * hb, n_halo - 1), ca + off))
    vec = pl.BlockSpec((1, c), lambda i: (0, 0))
    return pl.pallas_call(
        functools.partial(_conv_body, width=width, first_of_seq=first_of_seq,
                          last_of_seq=last_of_seq),
        grid=(nt // ROW_BLOCK,),
        in_specs=[cur(0), cur(1), prev(0), prev(1), nxt(0), nxt(1),
                  pl.BlockSpec((width, c), lambda i: (0, 0)), vec, vec, vec],
        out_specs=pl.BlockSpec((ROW_BLOCK, c), lambda i: (i, 0)),
        out_shape=jax.ShapeDtypeStruct((nt, c), BF16),
        scratch_shapes=[pltpu.VMEM((ROW_BLOCK + 2 * CONV_HALO, c), F32)],
        compiler_params=_params("parallel"),
        name="conv_module",
    )(z, z, z, z, z, z, conv_w, conv_b.reshape(1, c), ln_g.reshape(1, c), ln_b.reshape(1, c))


RW_ROWS = 128


def _head_sum(x, ones_bd):
    cols = [jnp.dot(x[:, j:j + LANES], ones_bd, precision=lax.Precision.HIGHEST,
                    preferred_element_type=F32) for j in range(0, x.shape[1], LANES)]
    return jnp.concatenate(cols, axis=1)


def _rw_pre_body(r_ref, k_ref, v_ref, l0_ref, l1_ref,
                 pr_ref, pk_ref, pv_ref, pl_ref, nr_ref, nk_ref, nv_ref, nl_ref,
                 mu_ref, w0_ref, a0_ref, w2_ref, a2_ref, kk_ref, ka_ref, rk_ref, bd_ref, ind_ref,
                 *out_refs, c, n_tanh, first_of_seq, last_of_seq):
    i = pl.program_id(0)
    row0 = i * RW_ROWS
    rows = lax.broadcasted_iota(jnp.int32, (RW_ROWS, 1), 0)
    ones_bd = bd_ref[...]
    ind = ind_ref[...]
    bonus = None
    for d in range(2):
        if d == 0:
            keep = jnp.where(first_of_seq(row0), 0.0, 1.0)
            halo = [ref[SUBLANES - 1:SUBLANES, :] * keep for ref in (pr_ref, pk_ref, pv_ref, pl_ref)]
            fix_row, amount = 0, 1
        else:
            keep = jnp.where(last_of_seq(row0), 0.0, 1.0)
            halo = [ref[0:1, :] * keep for ref in (nr_ref, nk_ref, nv_ref, nl_ref)]
            fix_row, amount = RW_ROWS - 1, RW_ROWS - 1

        def mixed(x, halo_row, mu):
            shifted = jnp.where(rows == fix_row, halo_row, pltpu.roll(x, amount, axis=0))
            return x + mu * (shifted - x)

        r = mixed(r_ref[...], halo[0], mu_ref[d, :, 0:c])
        k = mixed(k_ref[...], halo[1], mu_ref[d, :, c:2 * c])
        v = mixed(v_ref[...], halo[2], mu_ref[d, :, 2 * c:3 * c])
        low = mixed((l0_ref, l1_ref)[d][...], halo[3], mu_ref[d, :, 3 * c:3 * c + LANES])
        lane = lax.broadcasted_iota(jnp.int32, low.shape, 1)
        low = jnp.where(lane < n_tanh, jnp.tanh(low), low).astype(BF16)
        w_pre = w0_ref[d] + jnp.dot(low, w2_ref[d], preferred_element_type=F32)
        a_pre = a0_ref[d] + jnp.dot(low, a2_ref[d], preferred_element_type=F32)
        neg = -w_pre
        softplus = jnp.maximum(neg, 0.0) + jnp.log1p(jnp.exp(-jnp.abs(neg)))
        decay = jnp.exp(-jnp.exp(-softplus - 0.5))
        a = jax.nn.sigmoid(a_pre)
        kk = k * kk_ref[...]
        kk = kk / jnp.maximum(jnp.sqrt(_head_sum(kk * kk, ones_bd)), 1e-12)
        k = k * (1.0 + (a - 1.0) * ka_ref[...])
        beta = kk * a
        term = _head_sum(r * k * rk_ref[...], ones_bd) * v
        bonus = term if bonus is None else bonus + term
        o = out_refs[8 * d:8 * d + 8]
        o[0][...] = -kk
        o[1][...] = decay
        o[2][...] = beta
        o[3][...] = k
        o[4][...] = r * decay
        o[5][...] = v
        o[6][...] = jnp.dot(r * beta, ind, precision=lax.Precision.HIGHEST, preferred_element_type=F32)
        o[7][...] = jnp.dot(r * k, ind, precision=lax.Precision.HIGHEST, preferred_element_type=F32)
    out_refs[16][...] = bonus


def _rw_prepare(z, col_r, lp, first_of_seq, last_of_seq, ones_bd, ind):
    nt = z.shape[0]
    c = lp['rw_w0'].shape[1]
    assert lp['rw_w2'].shape[1] + lp['rw_a2'].shape[1] == LANES and col_r % c == 0
    cr = col_r // c
    cl = (col_r + 3 * c) // LANES
    hb = RW_ROWS // SUBLANES
    n_halo = nt // SUBLANES
    wide = lambda off: pl.BlockSpec((RW_ROWS, c), lambda i: (i, cr + off))
    low = lambda off: pl.BlockSpec((RW_ROWS, LANES), lambda i: (i, cl + off))
    prev_i = lambda i: jnp.maximum(i * hb - 1, 0)
    next_i = lambda i: jnp.minimum((i + 1) * hb, n_halo - 1)
    halo = lambda idx, width, col: pl.BlockSpec((SUBLANES, width), lambda i: (idx(i), col))
    full = lambda a: pl.BlockSpec(a.shape, lambda i: (0,) * a.ndim)
    consts = [lp['mu'], lp['w0'], lp['a0'], lp['w2s'], lp['a2s'], lp['k_k'], lp['k_a'], lp['r_k'],
              ones_bd, ind]
    wide_out = pl.BlockSpec((RW_ROWS, c), lambda i: (i, 0))
    head_out = pl.BlockSpec((RW_ROWS, LANES), lambda i: (i, 0))
    out_specs = ([wide_out] * 6 + [head_out] * 2) * 2 + [wide_out]
    out_shape = ([jax.ShapeDtypeStruct((nt, c), F32)] * 6
                 + [jax.ShapeDtypeStruct((nt, LANES), F32)] * 2) * 2 + [jax.ShapeDtypeStruct((nt, c), F32)]
    return pl.pallas_call(
        functools.partial(_rw_pre_body, c=c, n_tanh=lp['rw_w2'].shape[1], first_of_seq=first_of_seq,
                          last_of_seq=last_of_seq),
        grid=(nt // RW_ROWS,),
        in_specs=[wide(0), wide(1), wide(2), low(0), low(1),
                  halo(prev_i, c, cr), halo(prev_i, c, cr + 1), halo(prev_i, c, cr + 2),
                  halo(prev_i, LANES, cl),
                  halo(next_i, c, cr), halo(next_i, c, cr + 1), halo(next_i, c, cr + 2),
                  halo(next_i, LANES, cl + 1)] + [full(a) for a in consts],
        out_specs=out_specs,
        out_shape=out_shape,
        compiler_params=_params("parallel"),
        name="rwkv_prepare",
    )(*([z] * 13), *consts)


SCAN_STEPS = 32


def _scan_body(a_ref, w_ref, b_ref, k_ref, rw_ref, v_ref, rb_ref, rk_ref, s0_ref,
               y_ref, sfin_ref, h_ref, *, n_key):
    @pl.when(pl.program_id(1) == 0)
    def _():
        h_ref[...] = s0_ref[...]

    def step(t, carry):
        sa = h_ref[0] * a_ref[t, 0:1, :]
        yv = h_ref[0] * rw_ref[t, 0:1, :]
        for k in range(1, n_key):
            hk = h_ref[k]
            sa = sa + hk * a_ref[t, k:k + 1, :]
            yv = yv + hk * rw_ref[t, k:k + 1, :]
        vt = v_ref[t]
        y_ref[t] = yv + rb_ref[t] * sa + rk_ref[t] * vt
        for k in range(n_key):
            h_ref[k] = (h_ref[k] * w_ref[t, k:k + 1, :] + b_ref[t, k:k + 1, :] * sa
                        + k_ref[t, k:k + 1, :] * vt)
        return carry

    lax.fori_loop(0, a_ref.shape[0], step, 0)

    @pl.when(pl.program_id(1) == pl.num_programs(1) - 1)
    def _():
        sfin_ref[...] = h_ref[...]


def _scan(ops, s0):
    a, w, b, k, rw, v, rb, rk = ops
    t, n_key, lanes = a.shape
    n_val = v.shape[1]
    steps = min(SCAN_STEPS, t)
    key_spec = pl.BlockSpec((steps, n_key, LANES), lambda g, i: (i, 0, g))
    val_spec = pl.BlockSpec((steps, n_val, LANES), lambda g, i: (i, 0, g))
    one_spec = pl.BlockSpec((steps, 1, LANES), lambda g, i: (i, 0, g))
    state_spec = pl.BlockSpec((n_key, n_val, LANES), lambda g, i: (0, 0, g))
    return pl.pallas_call(
        functools.partial(_scan_body, n_key=n_key),
        grid=(lanes // LANES, t // steps),
        in_specs=[key_spec] * 5 + [val_spec, one_spec, one_spec, state_spec],
        out_specs=[val_spec, state_spec],
        out_shape=[jax.ShapeDtypeStruct((t, n_val, lanes), F32),
                   jax.ShapeDtypeStruct((n_key, n_val, lanes), F32)],
        scratch_shapes=[pltpu.VMEM((n_key, n_val, LANES), F32)],
        compiler_params=_params("parallel", "arbitrary"),
        name="rwkv_scan",
    )(a, w, b, k, rw, v, rb, rk, s0)


def _rw_post_body(yf_ref, yb_ref, bonus_ref, gl_ref, g2_ref, lg_ref, lb_ref, bd_ref, o_ref, *, head):
    ones_bd = bd_ref[...]
    y = yf_ref[...] + yb_ref[...]
    yc = y - _head_sum(y, ones_bd) * (1.0 / head)
    var = _head_sum(yc * yc, ones_bd) * (1.0 / head)
    yn = yc * lax.rsqrt(var + RWKV_GN_EPS) * lg_ref[...] + lb_ref[...]
    gate = jnp.dot(jax.nn.sigmoid(gl_ref[...]).astype(BF16), g2_ref[...], preferred_element_type=F32)
    o_ref[...] = ((yn + bonus_ref[...]) * gate).astype(o_ref.dtype)


def _rw_post(yf, yb, bonus, z, col_g, g2, ln_g, ln_b, ones_bd, head):
    nt, c = yf.shape
    wide = pl.BlockSpec((ROW_BLOCK, c), lambda i: (i, 0))
    vec = pl.BlockSpec((1, c), lambda i: (0, 0))
    return pl.pallas_call(
        functools.partial(_rw_post_body, head=head),
        grid=(nt // ROW_BLOCK,),
        in_specs=[wide, wide, wide,
                  pl.BlockSpec((ROW_BLOCK, LANES), lambda i: (i, col_g // LANES)),
                  pl.BlockSpec(g2.shape, lambda i: (0, 0)), vec, vec,
                  pl.BlockSpec(ones_bd.shape, lambda i: (0, 0))],
        out_specs=wide,
        out_shape=jax.ShapeDtypeStruct((nt, c), BF16),
        compiler_params=_params("parallel"),
        name="rwkv_output",
    )(yf, yb, bonus, z, g2, ln_g.reshape(1, c), ln_b.reshape(1, c), ones_bd)


def _lane_split(n_seq, n_val):
    split = 1
    while n_seq * split * 2 <= LANES and n_val // (split * 2) >= SUBLANES:
        split *= 2
    return split


def _pad_lanes(x):
    pad = -x.shape[-1] % LANES
    return jnp.pad(x, [(0, 0)] * (x.ndim - 1) + [(0, pad)]) if pad else x


def _to_scan(xf, xb, b, t, h, n, split, per_value):
    def one(x, rev):
        x = x.reshape(b, t, h, n)
        x = x[:, ::-1] if rev else x
        return x.transpose(1, 3, 0, 2).reshape(t, n, b * h)
    x = jnp.concatenate([one(xf, False), one(xb, True)], axis=-1)
    if per_value:
        x = x.reshape(t, split, n // split, 2 * b * h).transpose(0, 2, 1, 3)
        x = x.reshape(t, n // split, split * 2 * b * h)
    else:
        x = jnp.tile(x, (1, 1, split))
    return _pad_lanes(x)


def _head_scalars_to_scan(xf, xb, b, t, h, split):
    def one(x, rev):
        x = x[:, :h].reshape(b, t, h)
        x = x[:, ::-1] if rev else x
        return x.transpose(1, 0, 2).reshape(t, 1, b * h)
    x = jnp.concatenate([one(xf, False), one(xb, True)], axis=-1)
    return _pad_lanes(jnp.tile(x, (1, 1, split)))


def _from_scan(y, b, t, h, n, split):
    y = y[..., :split * 2 * b * h].reshape(t, n // split, split, 2, b, h)
    y = y.transpose(3, 4, 0, 5, 2, 1).reshape(2, b, t, h * n)
    return y[0].reshape(b * t, h * n), y[1][:, ::-1].reshape(b * t, h * n)


def _state_to_scan(sf, sb, split):
    b, h, n_val, n_key = sf.shape
    s = jnp.stack([sf, sb]).reshape(2, b, h, split, n_val // split, n_key)
    s = s.transpose(5, 4, 3, 0, 1, 2).reshape(n_key, n_val // split, split * 2 * b * h)
    return _pad_lanes(s)


def _state_from_scan(s, b, h, split):
    n_key, nv, _ = s.shape
    s = s[..., :split * 2 * b * h].reshape(n_key, nv, split, 2, b, h)
    s = s.transpose(3, 4, 5, 2, 1, 0).reshape(2, b, h, split * nv, n_key)
    return s[0], s[1]


def _run_scan(pre, rows, b, t, h, n, s0f, s0b):
    split = _lane_split(2 * b * h, n)
    fwd = [x[rows] for x in pre[0:8]]
    bwd = [x[rows] for x in pre[8:16]]
    ops = [_to_scan(fwd[j], bwd[j], b, t, h, n, split, per_value=(j == 5)) for j in range(6)]
    ops += [_head_scalars_to_scan(fwd[j], bwd[j], b, t, h, split) for j in (6, 7)]
    y, s_fin = _scan(ops, _state_to_scan(s0f, s0b, split))
    yf, yb = _from_scan(y, b, t, h, n, split)
    sf, sb = _state_from_scan(s_fin, b, h, split)
    return yf, yb, sf, sb


def _rope_tables(t, hd):
    quarter = hd // 4
    pos = jnp.arange(t, dtype=jnp.int32)
    row = (pos // GRID_W).astype(F32)
    col = (pos % GRID_W).astype(F32)
    inv_freq = jnp.power(ROPE_THETA, -jnp.arange(0, hd // 2, 2, dtype=F32) / (hd // 2))
    ang_r = row[:, None] * inv_freq[None, :]
    ang_c = col[:, None] * inv_freq[None, :]
    cr, sr, cc, sc = jnp.cos(ang_r), jnp.sin(ang_r), jnp.cos(ang_c), jnp.sin(ang_c)
    cos = jnp.concatenate([cr, cr, cc, cc], axis=-1)
    sin = jnp.concatenate([-sr, sr, -sc, sc], axis=-1)
    assert cos.shape == (t, 4 * quarter)
    ident = (jnp.ones((ROW_BLOCK, hd), F32), jnp.zeros((ROW_BLOCK, hd), F32))
    return jnp.concatenate([ident[0], cos]), jnp.concatenate([ident[1], sin])


def _pad_cols(w, n):
    return jnp.pad(w, ((0, 0), (0, n - w.shape[1])))


def kernel(x_prompt, x_sample, c, cache_k, cache_v, state_fwd, state_bwd, c_ctx, w_mod, b_mod, norm1, norm2, w_in, q_norm, k_norm, conv_w, conv_b, conv_ln_g, conv_ln_b, rw_mu, rw_w0, rw_w2, rw_a0, rw_a2, rw_g2, rw_k_k, rw_k_a, rw_r_k, rw_ln_g, rw_ln_b, w_out, ffn_gate, ffn_up, ffn_down):
    bp, tp, d = x_prompt.shape
    bs, ts, _ = x_sample.shape
    depth = w_in.shape[0]
    n_kv, hd = cache_k.shape[3], cache_k.shape[4]
    past = cache_k.shape[2]
    c_conv = conv_w.shape[2]
    c_rw = rw_w0.shape[2]
    h_rw, rw_head = rw_r_k.shape[1], rw_r_k.shape[2]
    c_attn = d - c_conv - c_rw
    n_heads = c_attn // hd
    grp = n_heads // n_kv
    n_in = w_in.shape[2]
    d_ff = ffn_gate.shape[2]
    np_rows = bp * tp
    nt = np_rows + bs * ts
    col_conv = c_attn + 2 * n_kv * hd
    col_rw = col_conv + 2 * c_conv
    col_g = col_rw + 3 * c_rw + 2 * LANES
    n_in_pad = -(-n_in // 512) * 512
    ff_pad = -(-d_ff // FF_ALIGN) * FF_ALIGN
    assert bs + 1 <= SUBLANES and tp % ROW_BLOCK == 0 and ts % ROW_BLOCK == 0
    assert hd == LANES and rw_head * 2 == LANES and conv_w.shape[1] // 2 <= CONV_HALO

    def group_of_row(r):
        return jnp.where(r < np_rows, 0, 1 + (r - np_rows) // ts)

    def rope_block_of_row(r):
        return jnp.where(r < np_rows, 0, 1 + ((r - np_rows) % ts) // ROW_BLOCK)

    def first_of_seq(r):
        return jnp.where(r < np_rows, r % tp == 0, (r - np_rows) % ts == 0)

    def last_of_seq_for(rows):
        def last_of_seq(r):
            e = r + rows
            return jnp.where(r < np_rows, e % tp == 0, (e - np_rows) % ts == 0)
        return last_of_seq

    cvec = jnp.zeros((SUBLANES, d), F32).at[0].set(c_ctx).at[1:1 + bs].set(c)
    mod_all = _modulation(cvec, w_mod, b_mod).reshape(depth, SUBLANES, 6, d)
    cos_tab, sin_tab = _rope_tables(ts, hd)
    lane_head = jnp.arange(LANES) // rw_head
    ones_bd = (lane_head[:, None] == lane_head[None, :]).astype(F32)
    ind = (jnp.arange(c_rw)[:, None] // rw_head == jnp.arange(LANES)[None, :]).astype(F32)
    zero_state = jnp.zeros((bp, h_rw, rw_head, rw_head), F32)

    tm = 1024 if nt % 1024 == 0 and np_rows % 1024 == 0 and ts % 1024 == 0 else ROW_BLOCK
    x = jnp.concatenate([x_prompt.reshape(np_rows, d), x_sample.reshape(bs * ts, d)], axis=0)
    ks_out, vs_out, sf_out, sb_out = [], [], [], []
    for l in range(depth):
        mod = mod_all[l]
        w_in_l = _pad_cols(w_in[l], n_in_pad).astype(BF16)
        half_low = rw_w2.shape[2]
        lp = {
            'mu': rw_mu[l].reshape(2, 1, -1), 'rw_w0': rw_w0[l], 'rw_w2': rw_w2[l], 'rw_a2': rw_a2[l],
            'w0': rw_w0[l].reshape(2, 1, c_rw), 'a0': rw_a0[l].reshape(2, 1, c_rw),
            'w2s': jnp.pad(rw_w2[l], ((0, 0), (0, LANES - half_low), (0, 0))).astype(BF16),
            'a2s': jnp.pad(rw_a2[l], ((0, 0), (half_low, 0), (0, 0))).astype(BF16),
            'k_k': rw_k_k[l].reshape(1, c_rw), 'k_a': rw_k_a[l].reshape(1, c_rw),
            'r_k': rw_r_k[l].reshape(1, c_rw),
        }

        h1 = _norm_mod(x, norm1[l], mod, group_of_row, shift_row=0, scale_row=1)
        z = _matmul(h1, w_in_l, F32, tm, 512)

        q, k_rot, k_normed, v_bf = _qk_norm_rope(z, q_norm[l], k_norm[l], cos_tab, sin_tab,
                                                  rope_block_of_row, n_heads, n_kv, hd)
        ks_out.append(k_normed[:np_rows].reshape(bp, tp, n_kv, hd))
        vs_out.append(z[:np_rows, c_attn + n_kv * hd:col_conv].reshape(bp, tp, n_kv, hd))
        attn_p = _attention(q[:np_rows].reshape(bp, tp, c_attn),
                            k_rot[:np_rows].reshape(bp, tp, n_kv * hd),
                            v_bf[:np_rows].reshape(bp, tp, n_kv * hd), grp, hd)
        k_all = jnp.concatenate([cache_k[:, l].reshape(bs, past, n_kv * hd).astype(BF16),
                                 k_rot[np_rows:].reshape(bs, ts, n_kv * hd)], axis=1)
        v_all = jnp.concatenate([cache_v[:, l].reshape(bs, past, n_kv * hd).astype(BF16),
                                 v_bf[np_rows:].reshape(bs, ts, n_kv * hd)], axis=1)
        attn_s = _attention(q[np_rows:].reshape(bs, ts, c_attn), k_all, v_all, grp, hd)
        attn = jnp.concatenate([attn_p.reshape(np_rows, c_attn), attn_s.reshape(bs * ts, c_attn)])

        conv = _conv_module(z, col_conv, conv_w[l], conv_b[l], conv_ln_g[l], conv_ln_b[l],
                            first_of_seq, last_of_seq_for(ROW_BLOCK))

        pre = _rw_prepare(z, col_rw, lp, first_of_seq, last_of_seq_for(RW_ROWS), ones_bd, ind)
        yf_p, yb_p, sf, sb = _run_scan(pre, slice(0, np_rows), bp, tp, h_rw, rw_head,
                                       zero_state, zero_state)
        yf_s, yb_s, _, _ = _run_scan(pre, slice(np_rows, nt), bs, ts, h_rw, rw_head,
                                     state_fwd[:, l], state_bwd[:, l])
        sf_out.append(sf)
        sb_out.append(sb)
        rw = _rw_post(jnp.concatenate([yf_p, yf_s]), jnp.concatenate([yb_p, yb_s]), pre[16], z,
                      col_g, rw_g2[l].astype(BF16), rw_ln_g[l], rw_ln_b[l], ones_bd, rw_head)

        mixed = jnp.concatenate([attn, conv, rw], axis=1)
        x = _matmul_gated_residual(mixed, w_out[l].astype(BF16), x, mod, group_of_row, 2, tm, 512)

        h2 = _norm_mod(x, norm2[l], mod, group_of_row, shift_row=3, scale_row=4)
        hidden = _matmul_swiglu(h2, _pad_cols(ffn_gate[l], ff_pad).astype(BF16),
                                _pad_cols(ffn_up[l], ff_pad).astype(BF16), tm, 512)
        w_down = jnp.pad(ffn_down[l], ((0, ff_pad - d_ff), (0, 0))).astype(BF16)
        x = _matmul_gated_residual(hidden, w_down, x, mod, group_of_row, 5, min(tm, 512), 256)

    return (x[:np_rows].reshape(bp, tp, d), x[np_rows:].reshape(bs, ts, d),
            jnp.stack(ks_out, axis=1), jnp.stack(vs_out, axis=1),
            jnp.stack(sf_out, axis=1), jnp.stack(sb_out, axis=1))
```

```python
import functools

import jax
import jax.numpy as jnp
from jax import lax
from jax.experimental import pallas as pl
from jax.experimental.pallas import tpu as pltpu

F32 = jnp.float32
BF16 = jnp.bfloat16

GRID_W = 64
ROPE_THETA = 10000.0
NORM_EPS = 1e-6
LN_EPS = 1e-5
RWKV_GN_EPS = 64e-5

LANES = 128
SUBLANES = 8
ROW_BLOCK = 256
CONV_HALO = 16
FF_ALIGN = 1024
VMEM_LIMIT = 56 * 1024 * 1024


def _params(*sem):
    return pltpu.CompilerParams(dimension_semantics=sem, vmem_limit_bytes=VMEM_LIMIT)


def _silu(x):
    return x * jax.nn.sigmoid(x)


def _mod_body(c_ref, w_ref, b_ref, o_ref):
    s = _silu(c_ref[...]).astype(BF16)
    o_ref[...] = jnp.dot(s, w_ref[...].astype(BF16), preferred_element_type=F32) + b_ref[...]


def _modulation(cvec, w_mod, b_mod):
    n_layer, d, n = w_mod.shape
    tn = 512
    return pl.pallas_call(
        _mod_body,
        grid=(n_layer, n // tn),
        in_specs=[pl.BlockSpec((SUBLANES, d), lambda l, j: (0, 0)),
                  pl.BlockSpec((None, d, tn), lambda l, j: (l, 0, j)),
                  pl.BlockSpec((None, 1, tn), lambda l, j: (l, 0, j))],
        out_specs=pl.BlockSpec((None, SUBLANES, tn), lambda l, j: (l, 0, j)),
        out_shape=jax.ShapeDtypeStruct((n_layer, SUBLANES, n), F32),
        compiler_params=_params("parallel", "parallel"),
        name="modulation",
    )(cvec, w_mod, b_mod.reshape(n_layer, 1, n))


def _norm_mod_body(x_ref, g_ref, mod_ref, o_ref, *, shift_row, scale_row):
    x = x_ref[...]
    y = x * lax.rsqrt(jnp.mean(x * x, axis=-1, keepdims=True) + NORM_EPS) * g_ref[...]
    sc = mod_ref[scale_row:scale_row + 1, :]
    sh = mod_ref[shift_row:shift_row + 1, :]
    o_ref[...] = (y * (1.0 + sc) + sh).astype(o_ref.dtype)


def _norm_mod(x, g, mod, group_of_row, shift_row, scale_row):
    nt, d = x.shape
    return pl.pallas_call(
        functools.partial(_norm_mod_body, shift_row=shift_row, scale_row=scale_row),
        grid=(nt // ROW_BLOCK,),
        in_specs=[pl.BlockSpec((ROW_BLOCK, d), lambda i: (i, 0)),
                  pl.BlockSpec((1, d), lambda i: (0, 0)),
                  pl.BlockSpec((None, 6, d), lambda i: (group_of_row(i * ROW_BLOCK), 0, 0))],
        out_specs=pl.BlockSpec((ROW_BLOCK, d), lambda i: (i, 0)),
        out_shape=jax.ShapeDtypeStruct((nt, d), BF16),
        compiler_params=_params("parallel"),
        name="norm_mod",
    )(x, g.reshape(1, d), mod)


def _mm_body(a_ref, b_ref, o_ref):
    o_ref[...] = jnp.dot(a_ref[...], b_ref[...], preferred_element_type=F32).astype(o_ref.dtype)


def _matmul(a, b, out_dtype, tm, tn):
    m, k = a.shape
    _, n = b.shape
    return pl.pallas_call(
        _mm_body,
        grid=(m // tm, n // tn),
        in_specs=[pl.BlockSpec((tm, k), lambda i, j: (i, 0)),
                  pl.BlockSpec((k, tn), lambda i, j: (0, j))],
        out_specs=pl.BlockSpec((tm, tn), lambda i, j: (i, j)),
        out_shape=jax.ShapeDtypeStruct((m, n), out_dtype),
        compiler_params=_params("parallel", "arbitrary"),
        name="matmul",
    )(a, b)


def _mm_resid_body(a_ref, b_ref, x_ref, mod_ref, o_ref, *, gate_row):
    acc = jnp.dot(a_ref[...], b_ref[...], preferred_element_type=F32)
    o_ref[...] = x_ref[...] + mod_ref[gate_row:gate_row + 1, :] * acc


def _matmul_gated_residual(a, b, x, mod, group_of_row, gate_row, tm, tn):
    m, k = a.shape
    _, n = b.shape
    return pl.pallas_call(
        functools.partial(_mm_resid_body, gate_row=gate_row),
        grid=(m // tm, n // tn),
        in_specs=[pl.BlockSpec((tm, k), lambda i, j: (i, 0)),
                  pl.BlockSpec((k, tn), lambda i, j: (0, j)),
                  pl.BlockSpec((tm, tn), lambda i, j: (i, j)),
                  pl.BlockSpec((None, 6, tn), lambda i, j: (group_of_row(i * tm), 0, j))],
        out_specs=pl.BlockSpec((tm, tn), lambda i, j: (i, j)),
        out_shape=jax.ShapeDtypeStruct((m, n), F32),
        compiler_params=_params("parallel", "arbitrary"),
        name="matmul_gated_residual",
    )(a, b, x, mod)


def _mm_swiglu_body(a_ref, wg_ref, wu_ref, o_ref):
    a = a_ref[...]
    g = jnp.dot(a, wg_ref[...], preferred_element_type=F32)
    u = jnp.dot(a, wu_ref[...], preferred_element_type=F32)
    o_ref[...] = (_silu(g) * u).astype(o_ref.dtype)


def _matmul_swiglu(a, wg, wu, tm, tn):
    m, k = a.shape
    _, n = wg.shape
    return pl.pallas_call(
        _mm_swiglu_body,
        grid=(m // tm, n // tn),
        in_specs=[pl.BlockSpec((tm, k), lambda i, j: (i, 0)),
                  pl.BlockSpec((k, tn), lambda i, j: (0, j)),
                  pl.BlockSpec((k, tn), lambda i, j: (0, j))],
        out_specs=pl.BlockSpec((tm, tn), lambda i, j: (i, j)),
        out_shape=jax.ShapeDtypeStruct((m, n), BF16),
        compiler_params=_params("parallel", "arbitrary"),
        name="matmul_swiglu",
    )(a, wg, wu)


def _qk_body(z_ref, qg_ref, kg_ref, cos_ref, sin_ref, q_ref, k_ref, kn_ref, v_ref,
             *, n_heads, n_kv, hd):
    cos = cos_ref[...]
    sin = sin_ref[...]
    lane = lax.broadcasted_iota(jnp.int32, (1, hd), 1)
    first_half = (lane // (hd // 4)) % 2 == 0

    def normed(col, g):
        x = z_ref[:, col:col + hd]
        return x * lax.rsqrt(jnp.mean(x * x, axis=-1, keepdims=True) + NORM_EPS) * g

    def rotated(y):
        swapped = jnp.where(first_half, pltpu.roll(y, hd - hd // 4, axis=1),
                            pltpu.roll(y, hd // 4, axis=1))
        return y * cos + swapped * sin

    for h in range(n_heads):
        y = rotated(normed(h * hd, qg_ref[...]))
        q_ref[:, h * hd:(h + 1) * hd] = (y * (hd ** -0.5)).astype(q_ref.dtype)
    for h in range(n_kv):
        y = normed((n_heads + h) * hd, kg_ref[...])
        kn_ref[:, h * hd:(h + 1) * hd] = y
        k_ref[:, h * hd:(h + 1) * hd] = rotated(y).astype(k_ref.dtype)
    v_col = (n_heads + n_kv) * hd
    v_ref[...] = z_ref[:, v_col:v_col + n_kv * hd].astype(v_ref.dtype)


def _qk_norm_rope(z, q_g, k_g, cos_tab, sin_tab, rope_block_of_row, n_heads, n_kv, hd):
    nt = z.shape[0]
    qkv_cols = (n_heads + 2 * n_kv) * hd
    return pl.pallas_call(
        functools.partial(_qk_body, n_heads=n_heads, n_kv=n_kv, hd=hd),
        grid=(nt // ROW_BLOCK,),
        in_specs=[pl.BlockSpec((ROW_BLOCK, qkv_cols), lambda i: (i, 0)),
                  pl.BlockSpec((1, hd), lambda i: (0, 0)),
                  pl.BlockSpec((1, hd), lambda i: (0, 0)),
                  pl.BlockSpec((ROW_BLOCK, hd), lambda i: (rope_block_of_row(i * ROW_BLOCK), 0)),
                  pl.BlockSpec((ROW_BLOCK, hd), lambda i: (rope_block_of_row(i * ROW_BLOCK), 0))],
        out_specs=[pl.BlockSpec((ROW_BLOCK, n_heads * hd), lambda i: (i, 0)),
                   pl.BlockSpec((ROW_BLOCK, n_kv * hd), lambda i: (i, 0)),
                   pl.BlockSpec((ROW_BLOCK, n_kv * hd), lambda i: (i, 0)),
                   pl.BlockSpec((ROW_BLOCK, n_kv * hd), lambda i: (i, 0))],
        out_shape=[jax.ShapeDtypeStruct((nt, n_heads * hd), BF16),
                   jax.ShapeDtypeStruct((nt, n_kv * hd), BF16),
                   jax.ShapeDtypeStruct((nt, n_kv * hd), F32),
                   jax.ShapeDtypeStruct((nt, n_kv * hd), BF16)],
        compiler_params=_params("parallel"),
        name="qk_norm_rope",
    )(z, q_g.reshape(1, hd), k_g.reshape(1, hd), cos_tab, sin_tab)


def _attn_body(q_ref, k_ref, v_ref, o_ref, *, grp, hd):
    k = k_ref[...]
    v = v_ref[...]
    for h in range(grp):
        q = q_ref[:, h * hd:(h + 1) * hd]
        s = lax.dot_general(q, k, (((1,), (1,)), ((), ())), preferred_element_type=F32)
        p = jnp.exp(s - jnp.max(s, axis=-1, keepdims=True))
        denom = jnp.sum(p, axis=-1, keepdims=True)
        o = jnp.dot(p.astype(v.dtype), v, preferred_element_type=F32) / denom
        o_ref[:, h * hd:(h + 1) * hd] = o.astype(o_ref.dtype)


def _attention(q, k, v, grp, hd):
    b, tq, qc = q.shape
    _, tk, kc = k.shape
    n_kv = kc // hd
    bq = min(ROW_BLOCK, tq)
    return pl.pallas_call(
        functools.partial(_attn_body, grp=grp, hd=hd),
        grid=(b, n_kv, tq // bq),
        in_specs=[pl.BlockSpec((None, bq, grp * hd), lambda bi, g, i: (bi, i, g)),
                  pl.BlockSpec((None, tk, hd), lambda bi, g, i: (bi, 0, g)),
                  pl.BlockSpec((None, tk, hd), lambda bi, g, i: (bi, 0, g))],
        out_specs=pl.BlockSpec((None, bq, grp * hd), lambda bi, g, i: (bi, i, g)),
        out_shape=jax.ShapeDtypeStruct((b, tq, qc), BF16),
        compiler_params=_params("parallel", "parallel", "arbitrary"),
        name="attention",
    )(q, k, v)


def _conv_body(za_ref, zb_ref, pa_ref, pb_ref, na_ref, nb_ref, w_ref, cb_ref, g_ref, b_ref,
               o_ref, buf_ref, *, width, first_of_seq, last_of_seq):
    i = pl.program_id(0)
    row0 = i * ROW_BLOCK
    keep_prev = jnp.where(first_of_seq(row0), 0.0, 1.0)
    keep_next = jnp.where(last_of_seq(row0), 0.0, 1.0)
    buf_ref[0:CONV_HALO, :] = pa_ref[...] * jax.nn.sigmoid(pb_ref[...]) * keep_prev
    buf_ref[CONV_HALO:CONV_HALO + ROW_BLOCK, :] = za_ref[...] * jax.nn.sigmoid(zb_ref[...])
    buf_ref[CONV_HALO + ROW_BLOCK:, :] = na_ref[...] * jax.nn.sigmoid(nb_ref[...]) * keep_next
    half = width // 2
    acc = jnp.zeros(o_ref.shape, F32) + cb_ref[...]
    for j in range(width):
        start = CONV_HALO - half + j
        acc = acc + buf_ref[start:start + ROW_BLOCK, :] * w_ref[j:j + 1, :]
    xc = acc - jnp.mean(acc, axis=-1, keepdims=True)
    var = jnp.mean(xc * xc, axis=-1, keepdims=True)
    y = xc * lax.rsqrt(var + LN_EPS) * g_ref[...] + b_ref[...]
    o_ref[...] = _silu(y).astype(o_ref.dtype)


def _conv_module(z, col_a, conv_w, conv_b, ln_g, ln_b, first_of_seq, last_of_seq):
    nt = z.shape[0]
    width, c = conv_w.shape
    ca = col_a // c
    hb = ROW_BLOCK // CONV_HALO
    n_halo = nt // CONV_HALO
    cur = lambda off: pl.BlockSpec((ROW_BLOCK, c), lambda i: (i, ca + off))
    prev = lambda off: pl.BlockSpec((CONV_HALO, c), lambda i: (jnp.maximum(i * hb - 1, 0), ca + off))
    nxt = lambda off: pl.BlockSpec((CONV_HALO, c),
                                   lambda i: (jnp.minimum((i + 1) * hb, n_halo - 1), ca + off))
    vec = pl.BlockSpec((1, c), lambda i: (0, 0))
    return pl.pallas_call(
        functools.partial(_conv_body, width=width, first_of_seq=first_of_seq,
                          last_of_seq=last_of_seq),
        grid=(nt // ROW_BLOCK,),
        in_specs=[cur(0), cur(1), prev(0), prev(1), nxt(0), nxt(1),
                  pl.BlockSpec((width, c), lambda i: (0, 0)), vec, vec, vec],
        out_specs=pl.BlockSpec((ROW_BLOCK, c), lambda i: (i, 0)),
        out_shape=jax.ShapeDtypeStruct((nt, c), BF16),
        scratch_shapes=[pltpu.VMEM((ROW_BLOCK + 2 * CONV_HALO, c), F32)],
        compiler_params=_params("parallel"),
        name="conv_module",
    )(z, z, z, z, z, z, conv_w, conv_b.reshape(1, c), ln_g.reshape(1, c), ln_b.reshape(1, c))


RW_ROWS = 128


def _head_sum(x, ones_bd):
    cols = [jnp.dot(x[:, j:j + LANES], ones_bd, precision=lax.Precision.HIGHEST,
                    preferred_element_type=F32) for j in range(0, x.shape[1], LANES)]
    return jnp.concatenate(cols, axis=1)


def _interleave_heads(a, b, head):
    lane = lax.broadcasted_iota(jnp.int32, (1, LANES), 1)
    low = lane < head
    tiles = []
    for j in range(0, a.shape[1], LANES):
        at, bt = a[:, j:j + LANES], b[:, j:j + LANES]
        tiles.append(jnp.where(low, at, pltpu.roll(bt, head, axis=1)))
        tiles.append(jnp.where(low, pltpu.roll(at, head, axis=1), bt))
    return jnp.concatenate(tiles, axis=1)


def _rw_pre_body(r_ref, k_ref, v_ref, l0_ref, l1_ref,
                 pr_ref, pk_ref, pv_ref, pl_ref, nr_ref, nk_ref, nv_ref, nl_ref,
                 mu_ref, w0_ref, a0_ref, w2_ref, a2_ref, kk_ref, ka_ref, rk_ref, bd_ref,
                 p1_ref, p2_ref, p3_ref, bonus_ref, *, c, head, n_tanh, row_off, first_of_seq,
                 last_of_seq):
    row0 = row_off + pl.program_id(0) * RW_ROWS
    rows = lax.broadcasted_iota(jnp.int32, (RW_ROWS, 1), 0)
    ones_bd = bd_ref[...]
    bonus = None
    for d in range(2):
        if d == 0:
            keep = jnp.where(first_of_seq(row0), 0.0, 1.0)
            halo = [ref[SUBLANES - 1:SUBLANES, :] * keep for ref in (pr_ref, pk_ref, pv_ref, pl_ref)]
            fix_row, amount = 0, 1
        else:
            keep = jnp.where(last_of_seq(row0), 0.0, 1.0)
            halo = [ref[0:1, :] * keep for ref in (nr_ref, nk_ref, nv_ref, nl_ref)]
            fix_row, amount = RW_ROWS - 1, RW_ROWS - 1

        def mixed(x, halo_row, mu):
            shifted = jnp.where(rows == fix_row, halo_row, pltpu.roll(x, amount, axis=0))
            return x + mu * (shifted - x)

        r = mixed(r_ref[...], halo[0], mu_ref[d, :, 0:c])
        k = mixed(k_ref[...], halo[1], mu_ref[d, :, c:2 * c])
        v = mixed(v_ref[...], halo[2], mu_ref[d, :, 2 * c:3 * c])
        low = mixed((l0_ref, l1_ref)[d][...], halo[3], mu_ref[d, :, 3 * c:3 * c + LANES])
        lane = lax.broadcasted_iota(jnp.int32, low.shape, 1)
        low = jnp.where(lane < n_tanh, jnp.tanh(low), low).astype(BF16)
        w_pre = w0_ref[d] + jnp.dot(low, w2_ref[d], preferred_element_type=F32)
        a_pre = a0_ref[d] + jnp.dot(low, a2_ref[d], preferred_element_type=F32)
        neg = -w_pre
        softplus = jnp.maximum(neg, 0.0) + jnp.log1p(jnp.exp(-jnp.abs(neg)))
        decay = jnp.exp(-jnp.exp(-softplus - 0.5))
        a = jax.nn.sigmoid(a_pre)
        kk = k * kk_ref[...]
        kk = kk / jnp.maximum(jnp.sqrt(_head_sum(kk * kk, ones_bd)), 1e-12)
        k = k * (1.0 + (a - 1.0) * ka_ref[...])
        term = _head_sum(r * k * rk_ref[...], ones_bd) * v
        bonus = term if bonus is None else bonus + term
        p1_ref[d] = _interleave_heads(-kk, decay, head)
        p2_ref[d] = _interleave_heads(kk * a, k, head)
        p3_ref[d] = _interleave_heads(r, v, head)
    bonus_ref[...] = bonus


def _rw_prepare(z, row_off, n_rows, col_r, lp, head, first_of_seq, last_of_seq, ones_bd):
    c = lp['w0'].shape[2]
    assert col_r % c == 0 and row_off % RW_ROWS == 0 and n_rows % RW_ROWS == 0
    cr = col_r // c
    cl = (col_r + 3 * c) // LANES
    ob = row_off // RW_ROWS
    hb = RW_ROWS // SUBLANES
    n_halo = z.shape[0] // SUBLANES
    wide = lambda off: pl.BlockSpec((RW_ROWS, c), lambda i: (ob + i, cr + off))
    low = lambda off: pl.BlockSpec((RW_ROWS, LANES), lambda i: (ob + i, cl + off))
    prev_i = lambda i: jnp.maximum((ob + i) * hb - 1, 0)
    next_i = lambda i: jnp.minimum((ob + i + 1) * hb, n_halo - 1)
    halo = lambda idx, width, col: pl.BlockSpec((SUBLANES, width), lambda i: (idx(i), col))
    full = lambda a: pl.BlockSpec(a.shape, lambda i: (0,) * a.ndim)
    consts = [lp['mu'], lp['w0'], lp['a0'], lp['w2s'], lp['a2s'], lp['k_k'], lp['k_a'], lp['r_k'],
              ones_bd]
    pair_out = pl.BlockSpec((2, RW_ROWS, 2 * c), lambda i: (0, i, 0))
    pair_shape = jax.ShapeDtypeStruct((2, n_rows, 2 * c), F32)
    return pl.pallas_call(
        functools.partial(_rw_pre_body, c=c, head=head, n_tanh=lp['n_tanh'], row_off=row_off,
                          first_of_seq=first_of_seq, last_of_seq=last_of_seq),
        grid=(n_rows // RW_ROWS,),
        in_specs=[wide(0), wide(1), wide(2), low(0), low(1),
                  halo(prev_i, c, cr), halo(prev_i, c, cr + 1), halo(prev_i, c, cr + 2),
                  halo(prev_i, LANES, cl),
                  halo(next_i, c, cr), halo(next_i, c, cr + 1), halo(next_i, c, cr + 2),
                  halo(next_i, LANES, cl + 1)] + [full(a) for a in consts],
        out_specs=[pair_out] * 3 + [pl.BlockSpec((RW_ROWS, c), lambda i: (i, 0))],
        out_shape=[pair_shape] * 3 + [jax.ShapeDtypeStruct((n_rows, c), F32)],
        compiler_params=_params("parallel"),
        name="rwkv_prepare",
    )(*([z] * 13), *consts)


SCAN_STEPS = 32


def _recurrence_step(ops_ref, val, h_ref, n_key):
    tiles = range(h_ref.shape[1] // SUBLANES)

    def row(j, k):
        return jnp.broadcast_to(ops_ref[j, k:k + 1, :], (SUBLANES, LANES))

    def rows_of(i):
        return slice(i * SUBLANES, (i + 1) * SUBLANES)

    sa = None
    for k in range(n_key):
        a = row(0, k)
        prod = [h_ref[k, rows_of(i), :] * a for i in tiles]
        sa = prod if sa is None else [s + p for s, p in zip(sa, prod)]
    y = None
    for k in range(n_key):
        w, b, kk, r = row(0, n_key + k), row(1, k), row(1, n_key + k), row(2, k)
        hn = [h_ref[k, rows_of(i), :] * w + b * sa[i] + kk * val[i] for i in tiles]
        for i in tiles:
            h_ref[k, rows_of(i), :] = hn[i]
        prod = [h * r for h in hn]
        y = prod if y is None else [s + p for s, p in zip(y, prod)]
    return y


def _run_steps(steps, stage, compute, flush, ops, stages):
    assert steps % 2 == 0
    stages[1][...] = jnp.zeros(stages[1].shape, F32)
    stage(0, ops[0])
    stage(1, ops[1])

    def pair(i, carry):
        t = 2 * i
        flush(jnp.maximum(t - 1, 0), stages[1])
        compute(ops[0], stages[0])
        stage(jnp.minimum(t + 2, steps - 1), ops[0])
        flush(t, stages[0])
        compute(ops[1], stages[1])
        stage(jnp.minimum(t + 3, steps - 1), ops[1])
        return carry

    lax.fori_loop(0, steps // 2, pair, 0)
    flush(steps - 1, stages[1])


def _scan_prompt_body(p1_ref, p2_ref, p3_ref, y_ref, sfin_ref, h_ref, ops_a, ops_b, out_a, out_b,
                      *, n_key):
    steps = p1_ref.shape[1]
    n_val = h_ref.shape[1]
    rev = pl.program_id(0) == 1

    @pl.when(pl.program_id(2) == 0)
    def _():
        h_ref[...] = jnp.zeros(h_ref.shape, F32)

    def src_time(t):
        return jnp.where(rev, steps - 1 - t, t)

    def stage(t, ops_ref):
        for j, ref in enumerate((p1_ref, p2_ref, p3_ref)):
            ops_ref[j] = ref[:, src_time(t)].reshape(LANES, LANES).T

    def flush(t, out_ref):
        y = out_ref[...].T[:, :n_val]
        y_ref[:, src_time(t)] = y.reshape(y_ref.shape[0], y_ref.shape[2], n_val)

    def compute(ops_ref, out_ref):
        val = [ops_ref[2, n_key + r:n_key + r + SUBLANES, :] for r in range(0, n_val, SUBLANES)]
        for i, y in enumerate(_recurrence_step(ops_ref, val, h_ref, n_key)):
            out_ref[i * SUBLANES:(i + 1) * SUBLANES, :] = y

    out_a[...] = jnp.zeros(out_a.shape, F32)
    _run_steps(steps, stage, compute, flush, (ops_a, ops_b), (out_a, out_b))

    @pl.when(pl.program_id(2) == pl.num_programs(2) - 1)
    def _():
        sfin_ref[...] = h_ref[...]


def _scan_prompt(p1, p2, p3, b, t, h, n_key):
    per = LANES // h
    assert b % per == 0 and 2 * n_key == LANES
    steps = min(SCAN_STEPS, t)
    nt = t // steps
    view = lambda p: p.reshape(2, b, t, h, LANES)
    time_block = lambda d, i: jnp.where(d == 0, i, nt - 1 - i)
    in_spec = pl.BlockSpec((None, per, steps, h, LANES), lambda d, g, i: (d, g, time_block(d, i), 0, 0))
    return pl.pallas_call(
        functools.partial(_scan_prompt_body, n_key=n_key),
        grid=(2, b // per, nt),
        in_specs=[in_spec] * 3,
        out_specs=[pl.BlockSpec((None, per, steps, h, n_key),
                                lambda d, g, i: (d, g, time_block(d, i), 0, 0)),
                   pl.BlockSpec((n_key, n_key, LANES), lambda d, g, i: (0, 0, d * (b // per) + g))],
        out_shape=[jax.ShapeDtypeStruct((2, b, t, h, n_key), F32),
                   jax.ShapeDtypeStruct((n_key, n_key, 2 * b * h), F32)],
        scratch_shapes=[pltpu.VMEM((n_key, n_key, LANES), F32),
                        pltpu.VMEM((3, LANES, LANES), F32), pltpu.VMEM((3, LANES, LANES), F32),
                        pltpu.VMEM((LANES, LANES), F32), pltpu.VMEM((LANES, LANES), F32)],
        compiler_params=_params("parallel", "parallel", "arbitrary"),
        name="rwkv_scan_prompt",
    )(view(p1), view(p2), view(p3))


def _scan_sample_body(f1_ref, f2_ref, f3_ref, b1_ref, b2_ref, b3_ref, s0_ref, yf_ref, yb_ref,
                      h_ref, ops_a, ops_b, out_a, out_b, *, n_key, split):
    steps = f1_ref.shape[1]
    n_val = h_ref.shape[1]
    per_dir = f1_ref.shape[0] * f1_ref.shape[2]
    n_seq = 2 * per_dir
    lane = lax.broadcasted_iota(jnp.int32, (1, LANES), 1)

    @pl.when(pl.program_id(0) == 0)
    def _():
        h_ref[...] = s0_ref[...]

    def stage(t, ops_ref):
        pairs = ((f1_ref, b1_ref), (f2_ref, b2_ref), (f3_ref, b3_ref))
        for j, (fr, br) in enumerate(pairs):
            f = fr[:, t].reshape(per_dir, LANES)
            b = br[:, steps - 1 - t].reshape(per_dir, LANES)
            ops_ref[j] = jnp.concatenate([f, b] * split, axis=0).T

    def flush(t, out_ref):
        y = out_ref[...].T
        shape = (yf_ref.shape[0], yf_ref.shape[2], split * n_val)
        yf_ref[:, t] = y[0:per_dir, :split * n_val].reshape(shape)
        yb_ref[:, steps - 1 - t] = y[per_dir:n_seq, :split * n_val].reshape(shape)

    def compute(ops_ref, out_ref):
        val = []
        for r in range(0, n_val, SUBLANES):
            tile = ops_ref[2, n_key + r:n_key + r + SUBLANES, :]
            for s in range(1, split):
                start = n_key + s * n_val + r
                tile = jnp.where(lane >= s * n_seq, ops_ref[2, start:start + SUBLANES, :], tile)
            val.append(tile)
        for i, y in enumerate(_recurrence_step(ops_ref, val, h_ref, n_key)):
            for s in range(split):
                moved = y if s == 0 else pltpu.roll(y, LANES - s * n_seq, axis=1)
                out_ref[s * n_val + i * SUBLANES:s * n_val + (i + 1) * SUBLANES, :] = moved

    out_a[...] = jnp.zeros(out_a.shape, F32)
    _run_steps(steps, stage, compute, flush, (ops_a, ops_b), (out_a, out_b))


def _scan_sample(p1, p2, p3, s0, b, t, h, n_key, split):
    assert split * 2 * b * h == LANES and 2 * n_key == LANES
    steps = min(SCAN_STEPS, t)
    nt = t // steps
    view = lambda p: p.reshape(2, b, t, h, LANES)
    fwd = pl.BlockSpec((None, b, steps, h, LANES), lambda i: (0, 0, i, 0, 0))
    bwd = pl.BlockSpec((None, b, steps, h, LANES), lambda i: (1, 0, nt - 1 - i, 0, 0))
    y_shape = jax.ShapeDtypeStruct((b, t, h, n_key), F32)
    return pl.pallas_call(
        functools.partial(_scan_sample_body, n_key=n_key, split=split),
        grid=(nt,),
        in_specs=[fwd] * 3 + [bwd] * 3 + [pl.BlockSpec(s0.shape, lambda i: (0, 0, 0))],
        out_specs=[pl.BlockSpec((b, steps, h, n_key), lambda i: (0, i, 0, 0)),
                   pl.BlockSpec((b, steps, h, n_key), lambda i: (0, nt - 1 - i, 0, 0))],
        out_shape=[y_shape, y_shape],
        scratch_shapes=[pltpu.VMEM(s0.shape, F32),
                        pltpu.VMEM((3, LANES, LANES), F32), pltpu.VMEM((3, LANES, LANES), F32),
                        pltpu.VMEM((LANES, LANES), F32), pltpu.VMEM((LANES, LANES), F32)],
        compiler_params=_params("arbitrary"),
        name="rwkv_scan_sample",
    )(view(p1), view(p2), view(p3), view(p1), view(p2), view(p3), s0)


def _rw_post_body(yf_ref, yb_ref, bonus_ref, gl_ref, g2_ref, lg_ref, lb_ref, bd_ref, o_ref, *, head):
    ones_bd = bd_ref[...]
    y = yf_ref[...] + yb_ref[...]
    yc = y - _head_sum(y, ones_bd) * (1.0 / head)
    var = _head_sum(yc * yc, ones_bd) * (1.0 / head)
    yn = yc * lax.rsqrt(var + RWKV_GN_EPS) * lg_ref[...] + lb_ref[...]
    gate = jnp.dot(jax.nn.sigmoid(gl_ref[...]).astype(BF16), g2_ref[...], preferred_element_type=F32)
    o_ref[...] = ((yn + bonus_ref[...]) * gate).astype(o_ref.dtype)


def _rw_post(yf, yb, bonus, z, row_off, col_g, g2, ln_g, ln_b, ones_bd, head):
    n_rows, c = yf.shape
    ob = row_off // ROW_BLOCK
    wide = pl.BlockSpec((ROW_BLOCK, c), lambda i: (i, 0))
    vec = pl.BlockSpec((1, c), lambda i: (0, 0))
    return pl.pallas_call(
        functools.partial(_rw_post_body, head=head),
        grid=(n_rows // ROW_BLOCK,),
        in_specs=[wide, wide, wide,
                  pl.BlockSpec((ROW_BLOCK, LANES), lambda i: (ob + i, col_g // LANES)),
                  pl.BlockSpec(g2.shape, lambda i: (0, 0)), vec, vec,
                  pl.BlockSpec(ones_bd.shape, lambda i: (0, 0))],
        out_specs=wide,
        out_shape=jax.ShapeDtypeStruct((n_rows, c), BF16),
        compiler_params=_params("parallel"),
        name="rwkv_output",
    )(yf, yb, bonus, z, g2, ln_g.reshape(1, c), ln_b.reshape(1, c), ones_bd)


def _state_to_scan(sf, sb, split):
    b, h, n_val, n_key = sf.shape
    s = jnp.stack([sf, sb]).reshape(2, b, h, split, n_val // split, n_key)
    return s.transpose(5, 4, 3, 0, 1, 2).reshape(n_key, n_val // split, split * 2 * b * h)


def _state_from_scan(s, b, h):
    n_key, n_val, _ = s.shape
    s = s.reshape(n_key, n_val, 2, b, h).transpose(2, 3, 4, 1, 0)
    return s[0], s[1]


def _rope_tables(t, hd):
    pos = jnp.arange(t, dtype=jnp.int32)
    row = (pos // GRID_W).astype(F32)
    col = (pos % GRID_W).astype(F32)
    inv_freq = jnp.power(ROPE_THETA, -jnp.arange(0, hd // 2, 2, dtype=F32) / (hd // 2))
    ang_r = row[:, None] * inv_freq[None, :]
    ang_c = col[:, None] * inv_freq[None, :]
    cr, sr, cc, sc = jnp.cos(ang_r), jnp.sin(ang_r), jnp.cos(ang_c), jnp.sin(ang_c)
    cos = jnp.concatenate([cr, cr, cc, cc], axis=-1)
    sin = jnp.concatenate([-sr, sr, -sc, sc], axis=-1)
    ident = (jnp.ones((ROW_BLOCK, hd), F32), jnp.zeros((ROW_BLOCK, hd), F32))
    return jnp.concatenate([ident[0], cos]), jnp.concatenate([ident[1], sin])


def _pad_cols(w, n):
    return jnp.pad(w, ((0, 0), (0, n - w.shape[1])))


def kernel(x_prompt, x_sample, c, cache_k, cache_v, state_fwd, state_bwd, c_ctx, w_mod, b_mod, norm1, norm2, w_in, q_norm, k_norm, conv_w, conv_b, conv_ln_g, conv_ln_b, rw_mu, rw_w0, rw_w2, rw_a0, rw_a2, rw_g2, rw_k_k, rw_k_a, rw_r_k, rw_ln_g, rw_ln_b, w_out, ffn_gate, ffn_up, ffn_down):
    bp, tp, d = x_prompt.shape
    bs, ts, _ = x_sample.shape
    depth = w_in.shape[0]
    n_kv, hd = cache_k.shape[3], cache_k.shape[4]
    past = cache_k.shape[2]
    c_conv = conv_w.shape[2]
    c_rw = rw_w0.shape[2]
    h_rw, rw_head = rw_r_k.shape[1], rw_r_k.shape[2]
    c_attn = d - c_conv - c_rw
    n_heads = c_attn // hd
    grp = n_heads // n_kv
    n_in = w_in.shape[2]
    d_ff = ffn_gate.shape[2]
    n_low = rw_w2.shape[2]
    np_rows = bp * tp
    ns_rows = bs * ts
    nt = np_rows + ns_rows
    col_conv = c_attn + 2 * n_kv * hd
    col_rw = col_conv + 2 * c_conv
    col_g = col_rw + 3 * c_rw + 2 * LANES
    n_in_pad = -(-n_in // 512) * 512
    ff_pad = -(-d_ff // FF_ALIGN) * FF_ALIGN
    split = LANES // (2 * bs * h_rw)
    assert bs + 1 <= SUBLANES and tp % ROW_BLOCK == 0 and ts % ROW_BLOCK == 0
    assert hd == LANES and rw_head * 2 == LANES and conv_w.shape[1] // 2 <= CONV_HALO
    assert n_low + rw_a2.shape[2] == LANES and rw_g2.shape[1] == LANES

    def group_of_row(r):
        return jnp.where(r < np_rows, 0, 1 + (r - np_rows) // ts)

    def rope_block_of_row(r):
        return jnp.where(r < np_rows, 0, 1 + ((r - np_rows) % ts) // ROW_BLOCK)

    def first_of_seq(r):
        return jnp.where(r < np_rows, r % tp == 0, (r - np_rows) % ts == 0)

    def last_of_seq_for(rows):
        def last_of_seq(r):
            e = r + rows
            return jnp.where(r < np_rows, e % tp == 0, (e - np_rows) % ts == 0)
        return last_of_seq

    cvec = jnp.zeros((SUBLANES, d), F32).at[0].set(c_ctx).at[1:1 + bs].set(c)
    mod_all = _modulation(cvec, w_mod, b_mod).reshape(depth, SUBLANES, 6, d)
    cos_tab, sin_tab = _rope_tables(ts, hd)
    lane_head = jnp.arange(LANES) // rw_head
    ones_bd = (lane_head[:, None] == lane_head[None, :]).astype(F32)

    tm = 1024 if nt % 1024 == 0 and np_rows % 1024 == 0 and ts % 1024 == 0 else ROW_BLOCK
    x = jnp.concatenate([x_prompt.reshape(np_rows, d), x_sample.reshape(ns_rows, d)], axis=0)
    ks_out, vs_out, sf_out, sb_out = [], [], [], []
    for l in range(depth):
        mod = mod_all[l]
        w_in_l = _pad_cols(w_in[l], n_in_pad).astype(BF16)
        lp = {
            'n_tanh': n_low,
            'mu': rw_mu[l].reshape(2, 1, -1),
            'w0': rw_w0[l].reshape(2, 1, c_rw), 'a0': rw_a0[l].reshape(2, 1, c_rw),
            'w2s': jnp.pad(rw_w2[l], ((0, 0), (0, LANES - n_low), (0, 0))).astype(BF16),
            'a2s': jnp.pad(rw_a2[l], ((0, 0), (n_low, 0), (0, 0))).astype(BF16),
            'k_k': rw_k_k[l].reshape(1, c_rw), 'k_a': rw_k_a[l].reshape(1, c_rw),
            'r_k': rw_r_k[l].reshape(1, c_rw),
        }

        h1 = _norm_mod(x, norm1[l], mod, group_of_row, shift_row=0, scale_row=1)
        z = _matmul(h1, w_in_l, F32, tm, 512)

        q, k_rot, k_normed, v_bf = _qk_norm_rope(z, q_norm[l], k_norm[l], cos_tab, sin_tab,
                                                  rope_block_of_row, n_heads, n_kv, hd)
        ks_out.append(k_normed[:np_rows].reshape(bp, tp, n_kv, hd))
        vs_out.append(z[:np_rows, c_attn + n_kv * hd:col_conv].reshape(bp, tp, n_kv, hd))
        attn_p = _attention(q[:np_rows].reshape(bp, tp, c_attn),
                            k_rot[:np_rows].reshape(bp, tp, n_kv * hd),
                            v_bf[:np_rows].reshape(bp, tp, n_kv * hd), grp, hd)
        k_all = jnp.concatenate([cache_k[:, l].reshape(bs, past, n_kv * hd).astype(BF16),
                                 k_rot[np_rows:].reshape(bs, ts, n_kv * hd)], axis=1)
        v_all = jnp.concatenate([cache_v[:, l].reshape(bs, past, n_kv * hd).astype(BF16),
                                 v_bf[np_rows:].reshape(bs, ts, n_kv * hd)], axis=1)
        attn_s = _attention(q[np_rows:].reshape(bs, ts, c_attn), k_all, v_all, grp, hd)
        attn = jnp.concatenate([attn_p.reshape(np_rows, c_attn), attn_s.reshape(ns_rows, c_attn)])

        conv = _conv_module(z, col_conv, conv_w[l], conv_b[l], conv_ln_g[l], conv_ln_b[l],
                            first_of_seq, last_of_seq_for(ROW_BLOCK))

        rw_parts = []
        for row_off, n_rows in ((0, np_rows), (np_rows, ns_rows)):
            p1, p2, p3, bonus = _rw_prepare(z, row_off, n_rows, col_rw, lp, rw_head, first_of_seq,
                                            last_of_seq_for(RW_ROWS), ones_bd)
            if row_off == 0:
                y, s_fin = _scan_prompt(p1, p2, p3, bp, tp, h_rw, rw_head)
                yf, yb = y[0].reshape(n_rows, c_rw), y[1].reshape(n_rows, c_rw)
                sf, sb = _state_from_scan(s_fin, bp, h_rw)
                sf_out.append(sf)
                sb_out.append(sb)
            else:
                s0 = _state_to_scan(state_fwd[:, l], state_bwd[:, l], split)
                yf, yb = _scan_sample(p1, p2, p3, s0, bs, ts, h_rw, rw_head, split)
                yf, yb = yf.reshape(n_rows, c_rw), yb.reshape(n_rows, c_rw)
            rw_parts.append(_rw_post(yf, yb, bonus, z, row_off, col_g, rw_g2[l].astype(BF16),
                                     rw_ln_g[l], rw_ln_b[l], ones_bd, rw_head))

        mixed = jnp.concatenate([attn, conv, jnp.concatenate(rw_parts)], axis=1)
        x = _matmul_gated_residual(mixed, w_out[l].astype(BF16), x, mod, group_of_row, 2, tm, 512)

        h2 = _norm_mod(x, norm2[l], mod, group_of_row, shift_row=3, scale_row=4)
        hidden = _matmul_swiglu(h2, _pad_cols(ffn_gate[l], ff_pad).astype(BF16),
                                _pad_cols(ffn_up[l], ff_pad).astype(BF16), tm, 512)
        w_down = jnp.pad(ffn_down[l], ((0, ff_pad - d_ff), (0, 0))).astype(BF16)
        x = _matmul_gated_residual(hidden, w_down, x, mod, group_of_row, 5, min(tm, 512), 256)

    return (x[:np_rows].reshape(bp, tp, d), x[np_rows:].reshape(bs, ts, d),
            jnp.stack(ks_out, axis=1), jnp.stack(vs_out, axis=1),
            jnp.stack(sf_out, axis=1), jnp.stack(sb_out, axis=1))
```

```python
import functools

import jax
import jax.numpy as jnp
from jax import lax
from jax.experimental import pallas as pl
from jax.experimental.pallas import tpu as pltpu

F32 = jnp.float32
BF16 = jnp.bfloat16

GRID_W = 64
ROPE_THETA = 10000.0
NORM_EPS = 1e-6
LN_EPS = 1e-5
RWKV_GN_EPS = 64e-5

LANES = 128
SUBLANES = 8
ROW_BLOCK = 256
CONV_HALO = 16
VMEM_LIMIT = 56 * 1024 * 1024


def _params(*sem):
    return pltpu.CompilerParams(dimension_semantics=sem, vmem_limit_bytes=VMEM_LIMIT)


def _silu(x):
    return x * jax.nn.sigmoid(x)


def _mod_body(c_ref, w_ref, b_ref, o_ref):
    s = _silu(c_ref[...]).astype(BF16)
    o_ref[...] = jnp.dot(s, w_ref[...].astype(BF16), preferred_element_type=F32) + b_ref[...]


def _modulation(cvec, w_mod, b_mod):
    n_layer, d, n = w_mod.shape
    tn = 512
    return pl.pallas_call(
        _mod_body,
        grid=(n_layer, n // tn),
        in_specs=[pl.BlockSpec((SUBLANES, d), lambda l, j: (0, 0)),
                  pl.BlockSpec((None, d, tn), lambda l, j: (l, 0, j)),
                  pl.BlockSpec((None, 1, tn), lambda l, j: (l, 0, j))],
        out_specs=pl.BlockSpec((None, SUBLANES, tn), lambda l, j: (l, 0, j)),
        out_shape=jax.ShapeDtypeStruct((n_layer, SUBLANES, n), F32),
        compiler_params=_params("parallel", "parallel"),
        name="modulation",
    )(cvec, w_mod, b_mod.reshape(n_layer, 1, n))


def _norm_mod_body(x_ref, g_ref, mod_ref, o_ref, *, shift_row, scale_row):
    x = x_ref[...]
    y = x * lax.rsqrt(jnp.mean(x * x, axis=-1, keepdims=True) + NORM_EPS) * g_ref[...]
    sc = mod_ref[scale_row:scale_row + 1, :]
    sh = mod_ref[shift_row:shift_row + 1, :]
    o_ref[...] = (y * (1.0 + sc) + sh).astype(o_ref.dtype)


def _norm_mod(x, g, mod, group_of_row, shift_row, scale_row):
    nt, d = x.shape
    return pl.pallas_call(
        functools.partial(_norm_mod_body, shift_row=shift_row, scale_row=scale_row),
        grid=(nt // ROW_BLOCK,),
        in_specs=[pl.BlockSpec((ROW_BLOCK, d), lambda i: (i, 0)),
                  pl.BlockSpec((1, d), lambda i: (0, 0)),
                  pl.BlockSpec((None, 6, d), lambda i: (group_of_row(i * ROW_BLOCK), 0, 0))],
        out_specs=pl.BlockSpec((ROW_BLOCK, d), lambda i: (i, 0)),
        out_shape=jax.ShapeDtypeStruct((nt, d), BF16),
        compiler_params=_params("parallel"),
        name="norm_mod",
    )(x, g.reshape(1, d), mod)


def _mm_body(a_ref, b_ref, o_ref):
    o_ref[...] = jnp.dot(a_ref[...], b_ref[...], preferred_element_type=F32).astype(o_ref.dtype)


def _matmul(a, b, out_dtype, tm, tn):
    m, k = a.shape
    _, n = b.shape
    return pl.pallas_call(
        _mm_body,
        grid=(m // tm, pl.cdiv(n, tn)),
        in_specs=[pl.BlockSpec((tm, k), lambda i, j: (i, 0)),
                  pl.BlockSpec((k, tn), lambda i, j: (0, j))],
        out_specs=pl.BlockSpec((tm, tn), lambda i, j: (i, j)),
        out_shape=jax.ShapeDtypeStruct((m, n), out_dtype),
        compiler_params=_params("parallel", "arbitrary"),
        name="matmul",
    )(a, b)


def _mm_resid_body(a_ref, b_ref, x_ref, mod_ref, o_ref, *, gate_row):
    acc = jnp.dot(a_ref[...], b_ref[...], preferred_element_type=F32)
    o_ref[...] = x_ref[...] + mod_ref[gate_row:gate_row + 1, :] * acc


def _matmul_gated_residual(a, b, x, mod, group_of_row, gate_row, tm, tn):
    m, k = a.shape
    _, n = b.shape
    return pl.pallas_call(
        functools.partial(_mm_resid_body, gate_row=gate_row),
        grid=(m // tm, pl.cdiv(n, tn)),
        in_specs=[pl.BlockSpec((tm, k), lambda i, j: (i, 0)),
                  pl.BlockSpec((k, tn), lambda i, j: (0, j)),
                  pl.BlockSpec((tm, tn), lambda i, j: (i, j)),
                  pl.BlockSpec((None, 6, tn), lambda i, j: (group_of_row(i * tm), 0, j))],
        out_specs=pl.BlockSpec((tm, tn), lambda i, j: (i, j)),
        out_shape=jax.ShapeDtypeStruct((m, n), F32),
        compiler_params=_params("parallel", "arbitrary"),
        name="matmul_gated_residual",
    )(a, b, x, mod)


def _mm_swiglu_body(a_ref, wg_ref, wu_ref, o_ref):
    a = a_ref[...]
    g = jnp.dot(a, wg_ref[...], preferred_element_type=F32)
    u = jnp.dot(a, wu_ref[...], preferred_element_type=F32)
    o_ref[...] = (_silu(g) * u).astype(o_ref.dtype)


def _matmul_swiglu(a, wg, wu, tm, tn):
    m, k = a.shape
    _, n = wg.shape
    return pl.pallas_call(
        _mm_swiglu_body,
        grid=(m // tm, pl.cdiv(n, tn)),
        in_specs=[pl.BlockSpec((tm, k), lambda i, j: (i, 0)),
                  pl.BlockSpec((k, tn), lambda i, j: (0, j)),
                  pl.BlockSpec((k, tn), lambda i, j: (0, j))],
        out_specs=pl.BlockSpec((tm, tn), lambda i, j: (i, j)),
        out_shape=jax.ShapeDtypeStruct((m, n), BF16),
        compiler_params=_params("parallel", "arbitrary"),
        name="matmul_swiglu",
    )(a, wg, wu)


def _qk_body(z_ref, qg_ref, kg_ref, cos_ref, sin_ref, q_ref, k_ref, kn_ref, v_ref,
             *, n_heads, n_kv, hd):
    cos = cos_ref[...]
    sin = sin_ref[...]
    lane = lax.broadcasted_iota(jnp.int32, (1, hd), 1)
    first_half = (lane // (hd // 4)) % 2 == 0

    def normed(col, g):
        x = z_ref[:, col:col + hd]
        return x * lax.rsqrt(jnp.mean(x * x, axis=-1, keepdims=True) + NORM_EPS) * g

    def rotated(y):
        swapped = jnp.where(first_half, pltpu.roll(y, hd - hd // 4, axis=1),
                            pltpu.roll(y, hd // 4, axis=1))
        return y * cos + swapped * sin

    for h in range(n_heads):
        y = rotated(normed(h * hd, qg_ref[...]))
        q_ref[:, h * hd:(h + 1) * hd] = (y * (hd ** -0.5)).astype(q_ref.dtype)
    for h in range(n_kv):
        y = normed((n_heads + h) * hd, kg_ref[...])
        kn_ref[:, h * hd:(h + 1) * hd] = y
        k_ref[:, h * hd:(h + 1) * hd] = rotated(y).astype(k_ref.dtype)
    v_col = (n_heads + n_kv) * hd
    v_ref[...] = z_ref[:, v_col:v_col + n_kv * hd].astype(v_ref.dtype)


def _qk_norm_rope(z, q_g, k_g, cos_tab, sin_tab, rope_block_of_row, n_heads, n_kv, hd):
    nt = z.shape[0]
    qkv_cols = (n_heads + 2 * n_kv) * hd
    return pl.pallas_call(
        functools.partial(_qk_body, n_heads=n_heads, n_kv=n_kv, hd=hd),
        grid=(nt // ROW_BLOCK,),
        in_specs=[pl.BlockSpec((ROW_BLOCK, qkv_cols), lambda i: (i, 0)),
                  pl.BlockSpec((1, hd), lambda i: (0, 0)),
                  pl.BlockSpec((1, hd), lambda i: (0, 0)),
                  pl.BlockSpec((ROW_BLOCK, hd), lambda i: (rope_block_of_row(i * ROW_BLOCK), 0)),
                  pl.BlockSpec((ROW_BLOCK, hd), lambda i: (rope_block_of_row(i * ROW_BLOCK), 0))],
        out_specs=[pl.BlockSpec((ROW_BLOCK, n_heads * hd), lambda i: (i, 0)),
                   pl.BlockSpec((ROW_BLOCK, n_kv * hd), lambda i: (i, 0)),
                   pl.BlockSpec((ROW_BLOCK, n_kv * hd), lambda i: (i, 0)),
                   pl.BlockSpec((ROW_BLOCK, n_kv * hd), lambda i: (i, 0))],
        out_shape=[jax.ShapeDtypeStruct((nt, n_heads * hd), BF16),
                   jax.ShapeDtypeStruct((nt, n_kv * hd), BF16),
                   jax.ShapeDtypeStruct((nt, n_kv * hd), F32),
                   jax.ShapeDtypeStruct((nt, n_kv * hd), BF16)],
        compiler_params=_params("parallel"),
        name="qk_norm_rope",
    )(z, q_g.reshape(1, hd), k_g.reshape(1, hd), cos_tab, sin_tab)


def _attn_body(q_ref, k_ref, v_ref, o_ref, *, grp, hd):
    k = k_ref[...]
    v = v_ref[...]
    for h in range(grp):
        q = q_ref[:, h * hd:(h + 1) * hd]
        s = lax.dot_general(q, k, (((1,), (1,)), ((), ())), preferred_element_type=F32)
        p = jnp.exp(s - jnp.max(s, axis=-1, keepdims=True))
        denom = jnp.sum(p, axis=-1, keepdims=True)
        o = jnp.dot(p.astype(v.dtype), v, preferred_element_type=F32) / denom
        o_ref[:, h * hd:(h + 1) * hd] = o.astype(o_ref.dtype)


def _attention(q, row_off, tq, k, v, grp, hd):
    qc = q.shape[1]
    b, tk, kc = k.shape
    n_kv = kc // hd
    bq = min(ROW_BLOCK, tq)
    rows = lambda first: (lambda bi, g, i: (first + bi * (tq // bq) + i, g))
    kv_spec = pl.BlockSpec((None, tk, hd), lambda bi, g, i: (bi, 0, g))
    return pl.pallas_call(
        functools.partial(_attn_body, grp=grp, hd=hd),
        grid=(b, n_kv, tq // bq),
        in_specs=[pl.BlockSpec((bq, grp * hd), rows(row_off // bq)), kv_spec, kv_spec],
        out_specs=pl.BlockSpec((bq, grp * hd), rows(0)),
        out_shape=jax.ShapeDtypeStruct((b * tq, qc), BF16),
        compiler_params=_params("parallel", "parallel", "arbitrary"),
        name="attention",
    )(q, k, v)


def _conv_body(za_ref, zb_ref, pa_ref, pb_ref, na_ref, nb_ref, w_ref, cb_ref, g_ref, b_ref,
               o_ref, buf_ref, *, width, first_of_seq, last_of_seq):
    i = pl.program_id(0)
    row0 = i * ROW_BLOCK
    keep_prev = jnp.where(first_of_seq(row0), 0.0, 1.0)
    keep_next = jnp.where(last_of_seq(row0), 0.0, 1.0)
    buf_ref[0:CONV_HALO, :] = pa_ref[...] * jax.nn.sigmoid(pb_ref[...]) * keep_prev
    buf_ref[CONV_HALO:CONV_HALO + ROW_BLOCK, :] = za_ref[...] * jax.nn.sigmoid(zb_ref[...])
    buf_ref[CONV_HALO + ROW_BLOCK:, :] = na_ref[...] * jax.nn.sigmoid(nb_ref[...]) * keep_next
    half = width // 2
    acc = jnp.zeros(o_ref.shape, F32) + cb_ref[...]
    for j in range(width):
        start = CONV_HALO - half + j
        acc = acc + buf_ref[start:start + ROW_BLOCK, :] * w_ref[j:j + 1, :]
    xc = acc - jnp.mean(acc, axis=-1, keepdims=True)
    var = jnp.mean(xc * xc, axis=-1, keepdims=True)
    y = xc * lax.rsqrt(var + LN_EPS) * g_ref[...] + b_ref[...]
    o_ref[...] = _silu(y).astype(o_ref.dtype)


def _conv_module(z, col_a, conv_w, conv_b, ln_g, ln_b, first_of_seq, last_of_seq):
    nt = z.shape[0]
    width, c = conv_w.shape
    ca = col_a // c
    hb = ROW_BLOCK // CONV_HALO
    n_halo = nt // CONV_HALO
    cur = lambda off: pl.BlockSpec((ROW_BLOCK, c), lambda i: (i, ca + off))
    prev = lambda off: pl.BlockSpec((CONV_HALO, c), lambda i: (jnp.maximum(i * hb - 1, 0), ca + off))
    nxt = lambda off: pl.BlockSpec((CONV_HALO, c),
                                   lambda i: (jnp.minimum((i + 1) * hb, n_halo - 1), ca + off))
    vec = pl.BlockSpec((1, c), lambda i: (0, 0))
    return pl.pallas_call(
        functools.partial(_conv_body, width=width, first_of_seq=first_of_seq,
                          last_of_seq=last_of_seq),
        grid=(nt // ROW_BLOCK,),
        in_specs=[cur(0), cur(1), prev(0), prev(1), nxt(0), nxt(1),
                  pl.BlockSpec((width, c), lambda i: (0, 0)), vec, vec, vec],
        out_specs=pl.BlockSpec((ROW_BLOCK, c), lambda i: (i, 0)),
        out_shape=jax.ShapeDtypeStruct((nt, c), BF16),
        scratch_shapes=[pltpu.VMEM((ROW_BLOCK + 2 * CONV_HALO, c), F32)],
        compiler_params=_params("parallel"),
        name="conv_module",
    )(z, z, z, z, z, z, conv_w, conv_b.reshape(1, c), ln_g.reshape(1, c), ln_b.reshape(1, c))


RW_ROWS = 128


def _head_sum(x, ones_bd):
    cols = [jnp.dot(x[:, j:j + LANES], ones_bd, precision=lax.Precision.HIGHEST,
                    preferred_element_type=F32) for j in range(0, x.shape[1], LANES)]
    return jnp.concatenate(cols, axis=1)


def _interleave_heads(a, b, head):
    lane = lax.broadcasted_iota(jnp.int32, (1, LANES), 1)
    low = lane < head
    tiles = []
    for j in range(0, a.shape[1], LANES):
        at, bt = a[:, j:j + LANES], b[:, j:j + LANES]
        tiles.append(jnp.where(low, at, pltpu.roll(bt, head, axis=1)))
        tiles.append(jnp.where(low, pltpu.roll(at, head, axis=1), bt))
    return jnp.concatenate(tiles, axis=1)


def _rw_pre_body(r_ref, k_ref, v_ref, l0_ref, l1_ref,
                 pr_ref, pk_ref, pv_ref, pl_ref, nr_ref, nk_ref, nv_ref, nl_ref,
                 mu_ref, w0_ref, a0_ref, w2_ref, a2_ref, kk_ref, ka_ref, rk_ref, bd_ref,
                 p1_ref, p2_ref, p3_ref, bonus_ref, *, c, head, n_tanh, row_off, first_of_seq,
                 last_of_seq):
    row0 = row_off + pl.program_id(0) * RW_ROWS
    rows = lax.broadcasted_iota(jnp.int32, (RW_ROWS, 1), 0)
    ones_bd = bd_ref[...]
    bonus = None
    for d in range(2):
        if d == 0:
            keep = jnp.where(first_of_seq(row0), 0.0, 1.0)
            halo = [ref[SUBLANES - 1:SUBLANES, :] * keep for ref in (pr_ref, pk_ref, pv_ref, pl_ref)]
            fix_row, amount = 0, 1
        else:
            keep = jnp.where(last_of_seq(row0), 0.0, 1.0)
            halo = [ref[0:1, :] * keep for ref in (nr_ref, nk_ref, nv_ref, nl_ref)]
            fix_row, amount = RW_ROWS - 1, RW_ROWS - 1

        def mixed(x, halo_row, mu):
            shifted = jnp.where(rows == fix_row, halo_row, pltpu.roll(x, amount, axis=0))
            return x + mu * (shifted - x)

        r = mixed(r_ref[...], halo[0], mu_ref[d, :, 0:c])
        k = mixed(k_ref[...], halo[1], mu_ref[d, :, c:2 * c])
        v = mixed(v_ref[...], halo[2], mu_ref[d, :, 2 * c:3 * c])
        low = mixed((l0_ref, l1_ref)[d][...], halo[3], mu_ref[d, :, 3 * c:3 * c + LANES])
        lane = lax.broadcasted_iota(jnp.int32, low.shape, 1)
        low = jnp.where(lane < n_tanh, jnp.tanh(low), low).astype(BF16)
        w_pre = w0_ref[d] + jnp.dot(low, w2_ref[d], preferred_element_type=F32)
        a_pre = a0_ref[d] + jnp.dot(low, a2_ref[d], preferred_element_type=F32)
        neg = -w_pre
        softplus = jnp.maximum(neg, 0.0) + jnp.log1p(jnp.exp(-jnp.abs(neg)))
        decay = jnp.exp(-jnp.exp(-softplus - 0.5))
        a = jax.nn.sigmoid(a_pre)
        kk = k * kk_ref[...]
        kk = kk / jnp.maximum(jnp.sqrt(_head_sum(kk * kk, ones_bd)), 1e-12)
        k = k * (1.0 + (a - 1.0) * ka_ref[...])
        term = _head_sum(r * k * rk_ref[...], ones_bd) * v
        bonus = term if bonus is None else bonus + term
        p1_ref[d] = _interleave_heads(-kk, decay, head)
        p2_ref[d] = _interleave_heads(kk * a, k, head)
        p3_ref[d] = _interleave_heads(r, v, head)
    bonus_ref[...] = bonus


def _rw_prepare(z, row_off, n_rows, col_r, lp, head, first_of_seq, last_of_seq, ones_bd):
    c = lp['w0'].shape[2]
    assert col_r % c == 0 and row_off % RW_ROWS == 0 and n_rows % RW_ROWS == 0
    cr = col_r // c
    cl = (col_r + 3 * c) // LANES
    ob = row_off // RW_ROWS
    hb = RW_ROWS // SUBLANES
    n_halo = z.shape[0] // SUBLANES
    wide = lambda off: pl.BlockSpec((RW_ROWS, c), lambda i: (ob + i, cr + off))
    low = lambda off: pl.BlockSpec((RW_ROWS, LANES), lambda i: (ob + i, cl + off))
    prev_i = lambda i: jnp.maximum((ob + i) * hb - 1, 0)
    next_i = lambda i: jnp.minimum((ob + i + 1) * hb, n_halo - 1)
    halo = lambda idx, width, col: pl.BlockSpec((SUBLANES, width), lambda i: (idx(i), col))
    full = lambda a: pl.BlockSpec(a.shape, lambda i: (0,) * a.ndim)
    consts = [lp['mu'], lp['w0'], lp['a0'], lp['w2s'], lp['a2s'], lp['k_k'], lp['k_a'], lp['r_k'],
              ones_bd]
    pair_out = pl.BlockSpec((2, RW_ROWS, 2 * c), lambda i: (0, i, 0))
    pair_shape = jax.ShapeDtypeStruct((2, n_rows, 2 * c), F32)
    return pl.pallas_call(
        functools.partial(_rw_pre_body, c=c, head=head, n_tanh=lp['n_tanh'], row_off=row_off,
                          first_of_seq=first_of_seq, last_of_seq=last_of_seq),
        grid=(n_rows // RW_ROWS,),
        in_specs=[wide(0), wide(1), wide(2), low(0), low(1),
                  halo(prev_i, c, cr), halo(prev_i, c, cr + 1), halo(prev_i, c, cr + 2),
                  halo(prev_i, LANES, cl),
                  halo(next_i, c, cr), halo(next_i, c, cr + 1), halo(next_i, c, cr + 2),
                  halo(next_i, LANES, cl + 1)] + [full(a) for a in consts],
        out_specs=[pair_out] * 3 + [pl.BlockSpec((RW_ROWS, c), lambda i: (i, 0))],
        out_shape=[pair_shape] * 3 + [jax.ShapeDtypeStruct((n_rows, c), F32)],
        compiler_params=_params("parallel"),
        name="rwkv_prepare",
    )(*([z] * 13), *consts)


SCAN_STEPS = 32


VAL_GROUP = 8


def _recurrence_step(ops_ref, val_row, h_ref, n_key, write_y):
    tile = (SUBLANES, LANES)
    key_tiles = [slice(k, k + SUBLANES) for k in range(0, n_key, SUBLANES)]

    def op(j, second, kt):
        return ops_ref[j, (n_key if second else 0) + kt.start:(n_key if second else 0) + kt.stop, :]

    for v0 in range(0, h_ref.shape[0], VAL_GROUP):
        vs = range(v0, v0 + VAL_GROUP)
        acc = None
        for kt in key_tiles:
            a = op(0, False, kt)
            prod = [h_ref[v, kt, :] * a for v in vs]
            acc = prod if acc is None else [s + p for s, p in zip(acc, prod)]
        sa = [jnp.broadcast_to(jnp.sum(s, axis=0, keepdims=True), tile) for s in acc]
        val = [val_row(v) for v in vs]
        acc = None
        for kt in key_tiles:
            w, b, kk, r = op(0, True, kt), op(1, False, kt), op(1, True, kt), op(2, False, kt)
            hn = [h_ref[v, kt, :] * w + b * sa[i] + kk * val[i] for i, v in enumerate(vs)]
            for i, v in enumerate(vs):
                h_ref[v, kt, :] = hn[i]
            prod = [h * r for h in hn]
            acc = prod if acc is None else [s + p for s, p in zip(acc, prod)]
        for i, v in enumerate(vs):
            write_y(v, jnp.sum(acc[i], axis=0, keepdims=True))


def _run_steps(steps, stage, compute, flush, ops, stages):
    assert steps % 2 == 0
    stages[1][...] = jnp.zeros(stages[1].shape, F32)
    stage(0, ops[0])
    stage(1, ops[1])

    def pair(i, carry):
        t = 2 * i
        flush(jnp.maximum(t - 1, 0), stages[1])
        compute(ops[0], stages[0])
        stage(jnp.minimum(t + 2, steps - 1), ops[0])
        flush(t, stages[0])
        compute(ops[1], stages[1])
        stage(jnp.minimum(t + 3, steps - 1), ops[1])
        return carry

    lax.fori_loop(0, steps // 2, pair, 0)
    flush(steps - 1, stages[1])


def _scan_prompt_body(p1_ref, p2_ref, p3_ref, y_ref, sfin_ref, h_ref, ops_a, ops_b, out_a, out_b,
                      *, n_key):
    steps = p1_ref.shape[1]
    n_val = h_ref.shape[0]
    rev = pl.program_id(0) == 1

    @pl.when(pl.program_id(2) == 0)
    def _():
        h_ref[...] = jnp.zeros(h_ref.shape, F32)

    def src_time(t):
        return jnp.where(rev, steps - 1 - t, t)

    def stage(t, ops_ref):
        for j, ref in enumerate((p1_ref, p2_ref, p3_ref)):
            ops_ref[j] = ref[:, src_time(t)].reshape(LANES, LANES).T

    def flush(t, out_ref):
        y = out_ref[...].T[:, :n_val]
        y_ref[:, src_time(t)] = y.reshape(y_ref.shape[0], y_ref.shape[2], n_val)

    def compute(ops_ref, out_ref):
        def val_row(v):
            return jnp.broadcast_to(ops_ref[2, n_key + v:n_key + v + 1, :], (SUBLANES, LANES))

        def write_y(v, row):
            out_ref[v:v + 1, :] = row

        _recurrence_step(ops_ref, val_row, h_ref, n_key, write_y)

    out_a[...] = jnp.zeros(out_a.shape, F32)
    _run_steps(steps, stage, compute, flush, (ops_a, ops_b), (out_a, out_b))

    @pl.when(pl.program_id(2) == pl.num_programs(2) - 1)
    def _():
        sfin_ref[...] = h_ref[...]


def _scan_prompt(p1, p2, p3, b, t, h, n_key):
    per = LANES // h
    assert b % per == 0 and 2 * n_key == LANES
    steps = min(SCAN_STEPS, t)
    nt = t // steps
    view = lambda p: p.reshape(2, b, t, h, LANES)
    time_block = lambda d, i: jnp.where(d == 0, i, nt - 1 - i)
    in_spec = pl.BlockSpec((None, per, steps, h, LANES), lambda d, g, i: (d, g, time_block(d, i), 0, 0))
    return pl.pallas_call(
        functools.partial(_scan_prompt_body, n_key=n_key),
        grid=(2, b // per, nt),
        in_specs=[in_spec] * 3,
        out_specs=[pl.BlockSpec((None, per, steps, h, n_key),
                                lambda d, g, i: (d, g, time_block(d, i), 0, 0)),
                   pl.BlockSpec((n_key, n_key, LANES), lambda d, g, i: (0, 0, d * (b // per) + g))],
        out_shape=[jax.ShapeDtypeStruct((2, b, t, h, n_key), F32),
                   jax.ShapeDtypeStruct((n_key, n_key, 2 * b * h), F32)],
        scratch_shapes=[pltpu.VMEM((n_key, n_key, LANES), F32),
                        pltpu.VMEM((3, LANES, LANES), F32), pltpu.VMEM((3, LANES, LANES), F32),
                        pltpu.VMEM((LANES, LANES), F32), pltpu.VMEM((LANES, LANES), F32)],
        compiler_params=_params("parallel", "parallel", "arbitrary"),
        name="rwkv_scan_prompt",
    )(view(p1), view(p2), view(p3))


def _scan_sample_body(f1_ref, f2_ref, f3_ref, b1_ref, b2_ref, b3_ref, s0_ref, yf_ref, yb_ref,
                      h_ref, ops_a, ops_b, out_a, out_b, *, n_key, split):
    steps = f1_ref.shape[1]
    n_val = h_ref.shape[0]
    per_dir = f1_ref.shape[0] * f1_ref.shape[2]
    n_seq = 2 * per_dir
    lane = lax.broadcasted_iota(jnp.int32, (1, LANES), 1)

    @pl.when(pl.program_id(0) == 0)
    def _():
        h_ref[...] = s0_ref[...]

    def stage(t, ops_ref):
        pairs = ((f1_ref, b1_ref), (f2_ref, b2_ref), (f3_ref, b3_ref))
        for j, (fr, br) in enumerate(pairs):
            f = fr[:, t].reshape(per_dir, LANES)
            b = br[:, steps - 1 - t].reshape(per_dir, LANES)
            ops_ref[j] = jnp.concatenate([f, b] * split, axis=0).T

    def flush(t, out_ref):
        y = out_ref[...].T
        shape = (yf_ref.shape[0], yf_ref.shape[2], split * n_val)
        yf_ref[:, t] = y[0:per_dir, :split * n_val].reshape(shape)
        yb_ref[:, steps - 1 - t] = y[per_dir:n_seq, :split * n_val].reshape(shape)

    def compute(ops_ref, out_ref):
        def val_row(v):
            row = ops_ref[2, n_key + v:n_key + v + 1, :]
            for s in range(1, split):
                other = ops_ref[2, n_key + s * n_val + v:n_key + s * n_val + v + 1, :]
                row = jnp.where(lane >= s * n_seq, other, row)
            return jnp.broadcast_to(row, (SUBLANES, LANES))

        def write_y(v, row):
            out_ref[v:v + 1, :] = row

        _recurrence_step(ops_ref, val_row, h_ref, n_key, write_y)
        for r in range(0, n_val, SUBLANES):
            for s in range(1, split):
                moved = pltpu.roll(out_ref[r:r + SUBLANES, :], LANES - s * n_seq, axis=1)
                out_ref[s * n_val + r:s * n_val + r + SUBLANES, :] = moved

    out_a[...] = jnp.zeros(out_a.shape, F32)
    _run_steps(steps, stage, compute, flush, (ops_a, ops_b), (out_a, out_b))


def _scan_sample(p1, p2, p3, s0, b, t, h, n_key, split):
    assert split * 2 * b * h == LANES and 2 * n_key == LANES
    steps = min(SCAN_STEPS, t)
    nt = t // steps
    view = lambda p: p.reshape(2, b, t, h, LANES)
    fwd = pl.BlockSpec((None, b, steps, h, LANES), lambda i: (0, 0, i, 0, 0))
    bwd = pl.BlockSpec((None, b, steps, h, LANES), lambda i: (1, 0, nt - 1 - i, 0, 0))
    y_shape = jax.ShapeDtypeStruct((b, t, h, n_key), F32)
    return pl.pallas_call(
        functools.partial(_scan_sample_body, n_key=n_key, split=split),
        grid=(nt,),
        in_specs=[fwd] * 3 + [bwd] * 3 + [pl.BlockSpec(s0.shape, lambda i: (0, 0, 0))],
        out_specs=[pl.BlockSpec((b, steps, h, n_key), lambda i: (0, i, 0, 0)),
                   pl.BlockSpec((b, steps, h, n_key), lambda i: (0, nt - 1 - i, 0, 0))],
        out_shape=[y_shape, y_shape],
        scratch_shapes=[pltpu.VMEM(s0.shape, F32),
                        pltpu.VMEM((3, LANES, LANES), F32), pltpu.VMEM((3, LANES, LANES), F32),
                        pltpu.VMEM((LANES, LANES), F32), pltpu.VMEM((LANES, LANES), F32)],
        compiler_params=_params("arbitrary"),
        name="rwkv_scan_sample",
    )(view(p1), view(p2), view(p3), view(p1), view(p2), view(p3), s0)


def _rw_post_body(yf_ref, yb_ref, bonus_ref, gl_ref, g2_ref, lg_ref, lb_ref, bd_ref, o_ref, *, head):
    ones_bd = bd_ref[...]
    y = yf_ref[...] + yb_ref[...]
    yc = y - _head_sum(y, ones_bd) * (1.0 / head)
    var = _head_sum(yc * yc, ones_bd) * (1.0 / head)
    yn = yc * lax.rsqrt(var + RWKV_GN_EPS) * lg_ref[...] + lb_ref[...]
    gate = jnp.dot(jax.nn.sigmoid(gl_ref[...]).astype(BF16), g2_ref[...], preferred_element_type=F32)
    o_ref[...] = ((yn + bonus_ref[...]) * gate).astype(o_ref.dtype)


def _rw_post(yf, yb, bonus, z, row_off, col_g, g2, ln_g, ln_b, ones_bd, head):
    n_rows, c = yf.shape
    ob = row_off // ROW_BLOCK
    wide = pl.BlockSpec((ROW_BLOCK, c), lambda i: (i, 0))
    vec = pl.BlockSpec((1, c), lambda i: (0, 0))
    return pl.pallas_call(
        functools.partial(_rw_post_body, head=head),
        grid=(n_rows // ROW_BLOCK,),
        in_specs=[wide, wide, wide,
                  pl.BlockSpec((ROW_BLOCK, LANES), lambda i: (ob + i, col_g // LANES)),
                  pl.BlockSpec(g2.shape, lambda i: (0, 0)), vec, vec,
                  pl.BlockSpec(ones_bd.shape, lambda i: (0, 0))],
        out_specs=wide,
        out_shape=jax.ShapeDtypeStruct((n_rows, c), BF16),
        compiler_params=_params("parallel"),
        name="rwkv_output",
    )(yf, yb, bonus, z, g2, ln_g.reshape(1, c), ln_b.reshape(1, c), ones_bd)


def _state_to_scan(sf, sb, split):
    b, h, n_val, n_key = sf.shape
    s = jnp.stack([sf, sb]).reshape(2, b, h, split, n_val // split, n_key)
    return s.transpose(4, 5, 3, 0, 1, 2).reshape(n_val // split, n_key, split * 2 * b * h)


def _state_from_scan(s, b, h):
    n_val, n_key, _ = s.shape
    s = s.reshape(n_val, n_key, 2, b, h).transpose(2, 3, 4, 0, 1)
    return s[0], s[1]


def _rope_tables(t, hd):
    pos = jnp.arange(t, dtype=jnp.int32)
    row = (pos // GRID_W).astype(F32)
    col = (pos % GRID_W).astype(F32)
    inv_freq = jnp.power(ROPE_THETA, -jnp.arange(0, hd // 2, 2, dtype=F32) / (hd // 2))
    ang_r = row[:, None] * inv_freq[None, :]
    ang_c = col[:, None] * inv_freq[None, :]
    cr, sr, cc, sc = jnp.cos(ang_r), jnp.sin(ang_r), jnp.cos(ang_c), jnp.sin(ang_c)
    cos = jnp.concatenate([cr, cr, cc, cc], axis=-1)
    sin = jnp.concatenate([-sr, sr, -sc, sc], axis=-1)
    ident = (jnp.ones((ROW_BLOCK, hd), F32), jnp.zeros((ROW_BLOCK, hd), F32))
    return jnp.concatenate([ident[0], cos]), jnp.concatenate([ident[1], sin])


def kernel(x_prompt, x_sample, c, cache_k, cache_v, state_fwd, state_bwd, c_ctx, w_mod, b_mod, norm1, norm2, w_in, q_norm, k_norm, conv_w, conv_b, conv_ln_g, conv_ln_b, rw_mu, rw_w0, rw_w2, rw_a0, rw_a2, rw_g2, rw_k_k, rw_k_a, rw_r_k, rw_ln_g, rw_ln_b, w_out, ffn_gate, ffn_up, ffn_down):
    bp, tp, d = x_prompt.shape
    bs, ts, _ = x_sample.shape
    depth = w_in.shape[0]
    n_kv, hd = cache_k.shape[3], cache_k.shape[4]
    past = cache_k.shape[2]
    c_conv = conv_w.shape[2]
    c_rw = rw_w0.shape[2]
    h_rw, rw_head = rw_r_k.shape[1], rw_r_k.shape[2]
    c_attn = d - c_conv - c_rw
    n_heads = c_attn // hd
    grp = n_heads // n_kv
    n_in = w_in.shape[2]
    d_ff = ffn_gate.shape[2]
    n_low = rw_w2.shape[2]
    np_rows = bp * tp
    ns_rows = bs * ts
    nt = np_rows + ns_rows
    col_conv = c_attn + 2 * n_kv * hd
    col_rw = col_conv + 2 * c_conv
    col_g = col_rw + 3 * c_rw + 2 * LANES
    split = LANES // (2 * bs * h_rw)
    assert bs + 1 <= SUBLANES and tp % ROW_BLOCK == 0 and ts % ROW_BLOCK == 0
    assert hd == LANES and rw_head * 2 == LANES and conv_w.shape[1] // 2 <= CONV_HALO
    assert n_low + rw_a2.shape[2] == LANES and rw_g2.shape[1] == LANES

    def group_of_row(r):
        return jnp.where(r < np_rows, 0, 1 + (r - np_rows) // ts)

    def rope_block_of_row(r):
        return jnp.where(r < np_rows, 0, 1 + ((r - np_rows) % ts) // ROW_BLOCK)

    def first_of_seq(r):
        return jnp.where(r < np_rows, r % tp == 0, (r - np_rows) % ts == 0)

    def last_of_seq_for(rows):
        def last_of_seq(r):
            e = r + rows
            return jnp.where(r < np_rows, e % tp == 0, (e - np_rows) % ts == 0)
        return last_of_seq

    cvec = jnp.zeros((SUBLANES, d), F32).at[0].set(c_ctx).at[1:1 + bs].set(c)
    mod_all = _modulation(cvec, w_mod, b_mod).reshape(depth, SUBLANES, 6, d)
    cos_tab, sin_tab = _rope_tables(ts, hd)
    lane_head = jnp.arange(LANES) // rw_head
    ones_bd = (lane_head[:, None] == lane_head[None, :]).astype(F32)

    tm = 1024 if nt % 1024 == 0 and np_rows % 1024 == 0 and ts % 1024 == 0 else ROW_BLOCK
    x = jnp.concatenate([x_prompt.reshape(np_rows, d), x_sample.reshape(ns_rows, d)], axis=0)
    ks_out, vs_out, sf_out, sb_out = [], [], [], []
    for l in range(depth):
        mod = mod_all[l]
        w_in_l = w_in[l].astype(BF16)
        lp = {
            'n_tanh': n_low,
            'mu': rw_mu[l].reshape(2, 1, -1),
            'w0': rw_w0[l].reshape(2, 1, c_rw), 'a0': rw_a0[l].reshape(2, 1, c_rw),
            'w2s': jnp.pad(rw_w2[l], ((0, 0), (0, LANES - n_low), (0, 0))).astype(BF16),
            'a2s': jnp.pad(rw_a2[l], ((0, 0), (n_low, 0), (0, 0))).astype(BF16),
            'k_k': rw_k_k[l].reshape(1, c_rw), 'k_a': rw_k_a[l].reshape(1, c_rw),
            'r_k': rw_r_k[l].reshape(1, c_rw),
        }

        h1 = _norm_mod(x, norm1[l], mod, group_of_row, shift_row=0, scale_row=1)
        z = _matmul(h1, w_in_l, F32, tm, 512)

        q, k_rot, k_normed, v_bf = _qk_norm_rope(z, q_norm[l], k_norm[l], cos_tab, sin_tab,
                                                  rope_block_of_row, n_heads, n_kv, hd)
        ks_out.append(k_normed[:np_rows].reshape(bp, tp, n_kv, hd))
        vs_out.append(z[:np_rows, c_attn + n_kv * hd:col_conv].reshape(bp, tp, n_kv, hd))
        attn_p = _attention(q, 0, tp, k_rot[:np_rows].reshape(bp, tp, n_kv * hd),
                            v_bf[:np_rows].reshape(bp, tp, n_kv * hd), grp, hd)
        k_all = jnp.concatenate([cache_k[:, l].reshape(bs, past, n_kv * hd).astype(BF16),
                                 k_rot[np_rows:].reshape(bs, ts, n_kv * hd)], axis=1)
        v_all = jnp.concatenate([cache_v[:, l].reshape(bs, past, n_kv * hd).astype(BF16),
                                 v_bf[np_rows:].reshape(bs, ts, n_kv * hd)], axis=1)
        attn = jnp.concatenate([attn_p, _attention(q, np_rows, ts, k_all, v_all, grp, hd)])

        conv = _conv_module(z, col_conv, conv_w[l], conv_b[l], conv_ln_g[l], conv_ln_b[l],
                            first_of_seq, last_of_seq_for(ROW_BLOCK))

        rw_parts = []
        for row_off, n_rows in ((0, np_rows), (np_rows, ns_rows)):
            p1, p2, p3, bonus = _rw_prepare(z, row_off, n_rows, col_rw, lp, rw_head, first_of_seq,
                                            last_of_seq_for(RW_ROWS), ones_bd)
            if row_off == 0:
                y, s_fin = _scan_prompt(p1, p2, p3, bp, tp, h_rw, rw_head)
                yf, yb = y[0].reshape(n_rows, c_rw), y[1].reshape(n_rows, c_rw)
                sf, sb = _state_from_scan(s_fin, bp, h_rw)
                sf_out.append(sf)
                sb_out.append(sb)
            else:
                s0 = _state_to_scan(state_fwd[:, l], state_bwd[:, l], split)
                yf, yb = _scan_sample(p1, p2, p3, s0, bs, ts, h_rw, rw_head, split)
                yf, yb = yf.reshape(n_rows, c_rw), yb.reshape(n_rows, c_rw)
            rw_parts.append(_rw_post(yf, yb, bonus, z, row_off, col_g, rw_g2[l].astype(BF16),
                                     rw_ln_g[l], rw_ln_b[l], ones_bd, rw_head))

        mixed = jnp.concatenate([attn, conv, jnp.concatenate(rw_parts)], axis=1)
        x = _matmul_gated_residual(mixed, w_out[l].astype(BF16), x, mod, group_of_row, 2, tm, 512)

        h2 = _norm_mod(x, norm2[l], mod, group_of_row, shift_row=3, scale_row=4)
        hidden = _matmul_swiglu(h2, ffn_gate[l].astype(BF16), ffn_up[l].astype(BF16), tm, 512)
        x = _matmul_gated_residual(hidden, ffn_down[l].astype(BF16), x, mod, group_of_row, 5,
                                   min(tm, 512), 512)

    return (x[:np_rows].reshape(bp, tp, d), x[np_rows:].reshape(bs, ts, d),
            jnp.stack(ks_out, axis=1), jnp.stack(vs_out, axis=1),
            jnp.stack(sf_out, axis=1), jnp.stack(sb_out, axis=1))
```
